```python
import math
import jax, jax.numpy as jnp
from jax import lax
import numpy as np

D_MODEL = 1024
BATCH = 8
SEQ = 2048
DEPTH = 1
DEC_BATCH = 128
DEC_SEQ = 1
PAST_LEN = 8192
PAGE_SIZE = 128

N_HEADS = 4
HEAD_DIM = 64
V_DIM = 2 * HEAD_DIM
QK_WIDTH = N_HEADS * 2 * HEAD_DIM
ATTN_WIDTH = N_HEADS * V_DIM
POOL_WINDOWS = (2, 4, 8, 16)
POOL_GROUPS = len(POOL_WINDOWS)
POOL_WIDTH = D_MODEL - ATTN_WIDTH
POOL_GROUP_DIM = POOL_WIDTH // POOL_GROUPS
POOL_BUF = max(POOL_WINDOWS) - 1
IN_WIDTH = 2 * QK_WIDTH + ATTN_WIDTH + POOL_WIDTH
D_FF = 2752
CONV_W = 3
PLE_DIM = 256
Q_BLOCK = 128
EPS = 1e-6
NEG = -1e30

kernel_name = "hymba_diffattn_pool_convffn_step"


def rmsnorm(x, g):
    xf = x.astype(jnp.float32)
    y = xf * lax.rsqrt(jnp.mean(xf * xf, axis=-1, keepdims=True) + EPS)
    return (y * g.astype(jnp.float32)).astype(x.dtype)


def lambda_init_fn(layer):
    return 0.8 - 0.6 * math.exp(-0.3 * layer)


def diff_lambda(lq1, lk1, lq2, lk2, lam_init):
    f = jnp.float32
    return (jnp.exp(jnp.sum(lq1.astype(f) * lk1.astype(f)))
            - jnp.exp(jnp.sum(lq2.astype(f) * lk2.astype(f))) + lam_init)


def project(h, ln1, w_in, g_q, g_k):
    B, S, _ = h.shape
    z = rmsnorm(h, ln1) @ w_in
    q = rmsnorm(z[..., :QK_WIDTH].reshape(B, S, N_HEADS, 2, HEAD_DIM), g_q)
    k = rmsnorm(z[..., QK_WIDTH:2 * QK_WIDTH].reshape(B, S, N_HEADS, 2, HEAD_DIM), g_k)
    v = z[..., 2 * QK_WIDTH:2 * QK_WIDTH + ATTN_WIDTH].reshape(B, S, N_HEADS, V_DIM)
    pin = z[..., 2 * QK_WIDTH + ATTN_WIDTH:]
    return q, k, v, pin


def diff_attn_core(q, k, v, q_pos, k_pos, lam):
    s = jnp.einsum('...qhcd,...khcd->...hcqk', q, k).astype(jnp.float32) * (HEAD_DIM ** -0.5)
    mask = k_pos[None, :] <= q_pos[:, None]
    s = jnp.where(mask, s, NEG)
    p = jax.nn.softmax(s, axis=-1)
    w = p[..., 0, :, :] - lam * p[..., 1, :, :]
    return jnp.einsum('...hqk,...khv->...qhv', w.astype(v.dtype), v)


def prompt_attn(q, k, v, lam):
    B, S = q.shape[:2]
    nb = S // Q_BLOCK
    qb = q.reshape(B, nb, Q_BLOCK, N_HEADS, 2, HEAD_DIM).transpose(1, 0, 2, 3, 4, 5)
    k_pos = jnp.arange(S)

    def blk(args):
        i, qi = args
        q_pos = i * Q_BLOCK + jnp.arange(Q_BLOCK)
        return diff_attn_core(qi, k, v, q_pos, k_pos, lam)

    o = lax.map(blk, (jnp.arange(nb), qb))
    return o.transpose(1, 0, 2, 3, 4).reshape(B, S, N_HEADS, V_DIM)


def sample_attn(q, k, v, cache_k, cache_v, layer, page_table, lam):
    n_pages = page_table.shape[1]
    past = n_pages * PAGE_SIZE
    sn = q.shape[1]
    q_pos = past + jnp.arange(sn)
    k_pos = jnp.arange(past + sn)

    def one(args):
        pt, qi, ki, vi = args
        kp = cache_k[layer, pt].reshape(past, N_HEADS, 2, HEAD_DIM)
        vp = cache_v[layer, pt].reshape(past, N_HEADS, V_DIM)
        kk = jnp.concatenate([kp, ki.astype(kp.dtype)], axis=0)
        vv = jnp.concatenate([vp, vi.astype(vp.dtype)], axis=0)
        return diff_attn_core(qi, kk, vv, q_pos, k_pos, lam)

    return lax.map(one, (page_table, q, k, v))


def pool_mixer(pin, buf, start, w_pool, pool_scale):
    B, S, _ = pin.shape
    cat = jnp.concatenate([buf.astype(pin.dtype), pin], axis=1)
    cs = jnp.pad(jnp.cumsum(cat.astype(jnp.float32), axis=1), ((0, 0), (1, 0), (0, 0)))
    pos = start + jnp.arange(S)
    means = []
    for g, w in enumerate(POOL_WINDOWS):
        sl = slice(g * POOL_GROUP_DIM, (g + 1) * POOL_GROUP_DIM)
        hi = cs[:, POOL_BUF + 1:POOL_BUF + 1 + S, sl]
        lo = cs[:, POOL_BUF + 1 - w:POOL_BUF + 1 - w + S, sl]
        cnt = jnp.minimum(w, pos + 1).astype(jnp.float32)[None, :, None]
        means.append((hi - lo) / cnt)
    mean = jnp.stack(means, axis=2)
    d = (mean - pin.astype(jnp.float32).reshape(B, S, POOL_GROUPS, POOL_GROUP_DIM)).astype(pin.dtype)
    y = jnp.einsum('bsgc,gce->bsge', d, w_pool).reshape(B, S, POOL_WIDTH) * pool_scale
    return y, cat[:, -POOL_BUF:]


def mix_out(att, pool, g_sub, lam_init, w_out):
    B, S = att.shape[:2]
    a = (rmsnorm(att, g_sub) * (1.0 - lam_init)).reshape(B, S, ATTN_WIDTH)
    return jnp.concatenate([a, pool.astype(a.dtype)], axis=-1) @ w_out


def conv_ffn(h, ln2, w_up, conv_w, conv_b, w_down, buf):
    u = rmsnorm(h, ln2) @ w_up
    S = u.shape[1]
    cat = jnp.concatenate([buf.astype(u.dtype), u], axis=1)
    c = conv_b + sum(cat[:, j:j + S] * conv_w[j] for j in range(CONV_W))
    g, val = jnp.split(c, 2, axis=-1)
    y = (jax.nn.gelu(g, approximate=False) * val) @ w_down
    return y, cat[:, -(CONV_W - 1):]


def ple(h, p, ln_ple, w_pg, w_pp):
    return h + jax.nn.sigmoid(rmsnorm(h, ln_ple) @ w_pg) * (p.astype(h.dtype) @ w_pp)


def setup_inputs(seed: int = 0) -> dict:
    key = jax.random.key(seed)
    ks = iter(jax.random.split(key, 48))
    f = jnp.float32

    def nrm(shape, scale=1.0):
        return jax.random.normal(next(ks), shape, f) * scale

    def gain(shape):
        return 1.0 + 0.02 * jax.random.normal(next(ks), shape, f)

    n_pages = PAST_LEN // PAGE_SIZE
    n_used = DEC_BATCH * n_pages
    n_pool = n_used + max(1, n_used // 4)
    page_table = jax.random.permutation(next(ks), n_pool)[:n_used].reshape(DEC_BATCH, n_pages).astype(jnp.int32)

    return {
        "x_prompt": nrm((BATCH, SEQ, D_MODEL)),
        "x_sample": nrm((DEC_BATCH, DEC_SEQ, D_MODEL)),
        "cache_k": nrm((DEPTH, n_pool, PAGE_SIZE, N_HEADS, 2, HEAD_DIM)),
        "cache_v": nrm((DEPTH, n_pool, PAGE_SIZE, N_HEADS, V_DIM)),
        "state_pool": nrm((DEPTH, DEC_BATCH, POOL_BUF, POOL_WIDTH)),
        "state_conv": nrm((DEPTH, DEC_BATCH, CONV_W - 1, 2 * D_FF)),
        "page_table": page_table,
        "p_prompt": nrm((DEPTH, BATCH, SEQ, PLE_DIM)),
        "p_sample": nrm((DEPTH, DEC_BATCH, DEC_SEQ, PLE_DIM)),
        "ln1": gain((DEPTH, D_MODEL)),
        "w_in": nrm((DEPTH, D_MODEL, IN_WIDTH), D_MODEL ** -0.5),
        "g_q": gain((DEPTH, HEAD_DIM)),
        "g_k": gain((DEPTH, HEAD_DIM)),
        "lam_q1": nrm((DEPTH, HEAD_DIM), 0.1),
        "lam_k1": nrm((DEPTH, HEAD_DIM), 0.1),
        "lam_q2": nrm((DEPTH, HEAD_DIM), 0.1),
        "lam_k2": nrm((DEPTH, HEAD_DIM), 0.1),
        "g_sub": gain((DEPTH, V_DIM)),
        "w_pool": nrm((DEPTH, POOL_GROUPS, POOL_GROUP_DIM, POOL_GROUP_DIM), POOL_GROUP_DIM ** -0.5),
        "pool_scale": gain((DEPTH, POOL_WIDTH)),
        "w_out": nrm((DEPTH, D_MODEL, D_MODEL), D_MODEL ** -0.5),
        "ln2": gain((DEPTH, D_MODEL)),
        "w_up": nrm((DEPTH, D_MODEL, 2 * D_FF), D_MODEL ** -0.5),
        "conv_w": nrm((DEPTH, CONV_W, 2 * D_FF), CONV_W ** -0.5),
        "conv_b": nrm((DEPTH, 2 * D_FF), 0.01),
        "w_down": nrm((DEPTH, D_FF, D_MODEL), D_FF ** -0.5),
        "ln_ple": gain((DEPTH, D_MODEL)),
        "w_pg": nrm((DEPTH, D_MODEL, D_MODEL), D_MODEL ** -0.5),
        "w_pp": nrm((DEPTH, PLE_DIM, D_MODEL), PLE_DIM ** -0.5),
    }


def reference(x_prompt, x_sample, cache_k, cache_v, state_pool, state_conv, page_table,
              p_prompt, p_sample, ln1, w_in, g_q, g_k, lam_q1, lam_k1, lam_q2, lam_k2,
              g_sub, w_pool, pool_scale, w_out, ln2, w_up, conv_w, conv_b, w_down,
              ln_ple, w_pg, w_pp):
    past = page_table.shape[1] * PAGE_SIZE
    bp = x_prompt.shape[0]
    h_p = x_prompt
    h_s = x_sample
    kp_l, vp_l, pp_l, cp_l = [], [], [], []
    ks_l, vs_l, ps_l, cs_l = [], [], [], []
    for l in range(DEPTH):
        lam_init = lambda_init_fn(l)
        lam = diff_lambda(lam_q1[l], lam_k1[l], lam_q2[l], lam_k2[l], lam_init)

        q, k, v, pin = project(h_p, ln1[l], w_in[l], g_q[l], g_k[l])
        att = prompt_attn(q, k, v, lam)
        pool, pbuf = pool_mixer(pin, jnp.zeros((bp, POOL_BUF, POOL_WIDTH), pin.dtype), 0,
                                w_pool[l], pool_scale[l])
        h_p = h_p + mix_out(att, pool, g_sub[l], lam_init, w_out[l])
        fo, cbuf = conv_ffn(h_p, ln2[l], w_up[l], conv_w[l], conv_b[l], w_down[l],
                            jnp.zeros((bp, CONV_W - 1, 2 * D_FF), h_p.dtype))
        h_p = h_p + fo
        h_p = ple(h_p, p_prompt[l], ln_ple[l], w_pg[l], w_pp[l])
        kp_l.append(k); vp_l.append(v); pp_l.append(pbuf); cp_l.append(cbuf)

        q, k, v, pin = project(h_s, ln1[l], w_in[l], g_q[l], g_k[l])
        att = sample_attn(q, k, v, cache_k, cache_v, l, page_table, lam)
        pool, pbuf = pool_mixer(pin, state_pool[l], past, w_pool[l], pool_scale[l])
        h_s = h_s + mix_out(att, pool, g_sub[l], lam_init, w_out[l])
        fo, cbuf = conv_ffn(h_s, ln2[l], w_up[l], conv_w[l], conv_b[l], w_down[l], state_conv[l])
        h_s = h_s + fo
        h_s = ple(h_s, p_sample[l], ln_ple[l], w_pg[l], w_pp[l])
        ks_l.append(k); vs_l.append(v); ps_l.append(pbuf); cs_l.append(cbuf)

    return (h_p, h_s,
            jnp.stack(kp_l), jnp.stack(vp_l), jnp.stack(pp_l), jnp.stack(cp_l),
            jnp.stack(ks_l), jnp.stack(vs_l), jnp.stack(ps_l), jnp.stack(cs_l))
```

```python
import functools
import math

import jax
import jax.numpy as jnp
from jax import lax
from jax.experimental import pallas as pl
from jax.experimental.pallas import tpu as pltpu

F32 = jnp.float32
BF16 = jnp.bfloat16

D_MODEL = 1024
N_HEADS = 4
HEAD_DIM = 64
V_DIM = 2 * HEAD_DIM
QK_WIDTH = N_HEADS * 2 * HEAD_DIM
ATTN_WIDTH = N_HEADS * V_DIM
POOL_WINDOWS = (2, 4, 8, 16)
POOL_WIDTH = D_MODEL - ATTN_WIDTH
POOL_GROUP_DIM = POOL_WIDTH // len(POOL_WINDOWS)
POOL_BUF = max(POOL_WINDOWS) - 1
IN_WIDTH = 2 * QK_WIDTH + ATTN_WIDTH + POOL_WIDTH
D_FF = 2752
CONV_W = 3
PLE_DIM = 256
PAGE_SIZE = 128
PAGE_ROWS = PAGE_SIZE * N_HEADS
EPS = 1e-6
NEG = -1e30

LANES = 128
SUBLANES = 8
MXU_DIM = 256
VMEM_LIMIT_BYTES = 56 * 1024 * 1024

FF_PAD = -(-D_FF // MXU_DIM) * MXU_DIM
FF_CHUNK = MXU_DIM
POOL_HALO = 16
CONV_HALO = SUBLANES


def _rms(x, g):
    ms = jnp.mean(x * x, axis=-1, keepdims=True)
    return x * lax.rsqrt(ms + EPS) * g


def _dot(a, b):
    return jnp.dot(a, b, preferred_element_type=F32)


def _dot_nt(a, b):
    return lax.dot_general(a, b, (((1,), (1,)), ((), ())), preferred_element_type=F32)


def _lambda(lamp_ref, lam_init):
    p = lamp_ref[...]
    s1 = jnp.sum(p[0:1] * p[1:2], axis=-1, keepdims=True)
    s2 = jnp.sum(p[2:3] * p[3:4], axis=-1, keepdims=True)
    return jnp.exp(s1) - jnp.exp(s2) + lam_init


def _proj_kernel(h_ref, ln1_ref, win_ref, gmat_ref, gq_ref, gk_ref,
                 q_ref, k_ref, v_ref, pin_ref, kb_ref, vb_ref, a_scr):
    a_scr[...] = _rms(h_ref[...], ln1_ref[...]).astype(BF16)
    gmat = gmat_ref[...]
    n_chunks = IN_WIDTH // MXU_DIM
    per_seg = QK_WIDTH // MXU_DIM
    for c in range(n_chunks):
        z = _dot(a_scr[...], win_ref[:, c * MXU_DIM:(c + 1) * MXU_DIM])
        seg, off = divmod(c, per_seg)
        sl = slice(off * MXU_DIM, (off + 1) * MXU_DIM)
        if seg < 2:
            ms = _dot((z * z).astype(BF16), gmat)
            y = z * lax.rsqrt(ms + EPS)
            if seg == 0:
                q_ref[:, sl] = (y * gq_ref[:, sl]).astype(BF16)
            else:
                y = y * gk_ref[:, sl]
                k_ref[:, sl] = y
                kb_ref[:, sl] = y.astype(BF16)
        elif seg == 2:
            v_ref[:, sl] = z
            vb_ref[:, sl] = z.astype(BF16)
        else:
            pin_ref[:, sl] = z


def _project(h2d, ln1, win_b, gmat, gq, gk, tm):
    t = h2d.shape[0]
    assert t % tm == 0
    row = lambda i: (i, 0)
    const = lambda i: (0, 0)
    w512 = pl.BlockSpec((tm, QK_WIDTH), row)
    out_shape = (
        jax.ShapeDtypeStruct((t, QK_WIDTH), BF16),
        jax.ShapeDtypeStruct((t, QK_WIDTH), F32),
        jax.ShapeDtypeStruct((t, ATTN_WIDTH), F32),
        jax.ShapeDtypeStruct((t, POOL_WIDTH), F32),
        jax.ShapeDtypeStruct((t, QK_WIDTH), BF16),
        jax.ShapeDtypeStruct((t, ATTN_WIDTH), BF16),
    )
    return pl.pallas_call(
        _proj_kernel,
        grid=(t // tm,),
        in_specs=[
            pl.BlockSpec((tm, D_MODEL), row),
            pl.BlockSpec((1, D_MODEL), const),
            pl.BlockSpec((D_MODEL, IN_WIDTH), const, pipeline_mode=pl.Buffered(1)),
            pl.BlockSpec((MXU_DIM, MXU_DIM), const),
            pl.BlockSpec((1, QK_WIDTH), const),
            pl.BlockSpec((1, QK_WIDTH), const),
        ],
        out_specs=(w512,) * 6,
        out_shape=out_shape,
        scratch_shapes=[pltpu.VMEM((tm, D_MODEL), BF16)],
        compiler_params=pltpu.CompilerParams(
            dimension_semantics=("arbitrary",), vmem_limit_bytes=VMEM_LIMIT_BYTES),
        name="proj",
    )(h2d, ln1, win_b, gmat, gq, gk)


def _attn_kernel(lamp_ref, q_ref, k_ref, v_ref, o_ref, m_scr, l_scr, acc_scr, *, blk, lam_init):
    i = pl.program_id(1)
    lam = _lambda(lamp_ref, lam_init)
    lane = lax.broadcasted_iota(jnp.int32, (blk, V_DIM), 1)
    row = lax.broadcasted_iota(jnp.int32, (2 * blk, blk), 0)
    col = lax.broadcasted_iota(jnp.int32, (2 * blk, blk), 1)
    causal = col <= jnp.where(row >= blk, row - blk, row)
    reps = blk // LANES

    for h in range(N_HEADS):
        hs = slice(h * V_DIM, (h + 1) * V_DIM)
        qh = q_ref[0, :, hs]
        zero = jnp.zeros_like(qh)
        qq = jnp.concatenate([jnp.where(lane < HEAD_DIM, qh, zero),
                              jnp.where(lane >= HEAD_DIM, qh, zero)], axis=0)
        m_scr[...] = jnp.full(m_scr.shape, -jnp.inf, F32)
        l_scr[...] = jnp.zeros(l_scr.shape, F32)
        acc_scr[...] = jnp.zeros(acc_scr.shape, F32)

        def step(j, masked, hs=hs, qq=qq):
            start = pl.multiple_of(j * blk, blk)
            ks = k_ref[0, pl.ds(start, blk), hs]
            vs = v_ref[0, pl.ds(start, blk), hs]
            s = _dot_nt(qq, ks)
            if masked:
                s = jnp.where(causal, s, NEG)
            m_prev = m_scr[...]
            m_new = jnp.maximum(m_prev, jnp.max(s, axis=1, keepdims=True))
            alpha = jnp.exp(m_prev - m_new)
            p = jnp.exp(s - jnp.concatenate([m_new] * reps, axis=1))
            l_scr[...] = alpha * l_scr[...] + jnp.sum(p, axis=1, keepdims=True)
            acc_scr[...] = alpha * acc_scr[...] + _dot(p.astype(BF16), vs)
            m_scr[...] = m_new

        def body(j, carry):
            step(j, False)
            return carry

        lax.fori_loop(0, i, body, 0)
        step(i, True)
        o = acc_scr[...] / l_scr[...]
        o_ref[0, :, hs] = (o[:blk] - lam * o[blk:]).astype(o_ref.dtype)


def _prompt_attention(lamp, q, kb, vb, lam_init, blk):
    b, s, _ = q.shape
    assert s % blk == 0
    kernel = functools.partial(_attn_kernel, blk=blk, lam_init=lam_init)
    return pl.pallas_call(
        kernel,
        grid=(b, s // blk),
        in_specs=[
            pl.BlockSpec((4, HEAD_DIM), lambda bi, i: (0, 0)),
            pl.BlockSpec((1, blk, QK_WIDTH), lambda bi, i: (bi, i, 0)),
            pl.BlockSpec((1, s, QK_WIDTH), lambda bi, i: (bi, 0, 0)),
            pl.BlockSpec((1, s, ATTN_WIDTH), lambda bi, i: (bi, 0, 0)),
        ],
        out_specs=pl.BlockSpec((1, blk, ATTN_WIDTH), lambda bi, i: (bi, i, 0)),
        out_shape=jax.ShapeDtypeStruct((b, s, ATTN_WIDTH), BF16),
        scratch_shapes=[pltpu.VMEM((2 * blk, V_DIM), F32)] * 3,
        compiler_params=pltpu.CompilerParams(
            dimension_semantics=("arbitrary", "arbitrary"), vmem_limit_bytes=VMEM_LIMIT_BYTES),
        name="prompt_attn",
    )(lamp, q, kb, vb)


def _decode_kernel(pt_ref, lamp_ref, q_ref, ks_ref, vs_ref, ck_hbm, cv_hbm, o_ref,
                   kbuf, vbuf, sem, qbd_scr, m_scr, l_scr, acc_scr, *, ppc, n_req, n_pages, lam_init):
    r = pl.program_id(0)
    c = pl.program_id(1)
    n_chunks = n_pages // ppc
    step = r * n_chunks + c
    slot = lax.rem(step, 2)

    def page_copies(req, chunk, slot_, page_of):
        copies = []
        for p in range(ppc):
            page = page_of(req * n_pages + chunk * ppc + p)
            dst = pl.ds(p * PAGE_ROWS, PAGE_ROWS)
            copies.append(pltpu.make_async_copy(ck_hbm.at[page], kbuf.at[slot_, p], sem.at[slot_, 0]))
            copies.append(pltpu.make_async_copy(cv_hbm.at[page], vbuf.at[slot_, dst], sem.at[slot_, 1]))
        return copies

    @pl.when(step == 0)
    def _():
        for cp in page_copies(0, 0, 0, lambda idx: pt_ref[idx]):
            cp.start()

    @pl.when(step + 1 < n_req * n_chunks)
    def _():
        nxt = step + 1
        for cp in page_copies(lax.div(nxt, n_chunks), lax.rem(nxt, n_chunks), 1 - slot, lambda idx: pt_ref[idx]):
            cp.start()

    hc = lax.broadcasted_iota(jnp.int32, (2 * N_HEADS, QK_WIDTH), 0)
    grp = lax.broadcasted_iota(jnp.int32, (2 * N_HEADS, QK_WIDTH), 1) // HEAD_DIM

    @pl.when(c == 0)
    def _():
        q_row = q_ref[pl.ds(r, 1), :]
        qbd = jnp.where(hc == grp, jnp.broadcast_to(q_row, hc.shape), 0.0)
        qbd_scr[...] = qbd.astype(BF16)
        s_self = jnp.sum(qbd * ks_ref[pl.ds(r, 1), :], axis=-1, keepdims=True)
        m_scr[...] = jnp.broadcast_to(s_self, m_scr.shape)
        l_scr[...] = jnp.ones(l_scr.shape, F32)
        acc_scr[...] = jnp.broadcast_to(vs_ref[pl.ds(r, 1), :], acc_scr.shape)

    for cp in page_copies(0, 0, slot, lambda idx: 0):
        cp.wait()

    qbd = qbd_scr[...]
    s = jnp.concatenate([_dot(qbd, kbuf[slot, p].astype(BF16)) for p in range(ppc)], axis=1)
    m_prev = m_scr[...]
    m_new = jnp.maximum(m_prev, jnp.max(s, axis=1, keepdims=True))
    alpha = jnp.exp(m_prev - m_new)
    p = jnp.exp(s - m_new[:, :1])
    l_scr[...] = alpha * l_scr[...] + jnp.sum(p, axis=1, keepdims=True)
    pb = p.astype(BF16)
    pv = [_dot(pb, vbuf[slot, pl.ds(h, ppc * PAGE_SIZE, stride=N_HEADS), :].astype(BF16))
          for h in range(N_HEADS)]
    acc_scr[...] = alpha[:, :1] * acc_scr[...] + jnp.concatenate(pv, axis=1)
    m_scr[...] = m_new

    @pl.when(c == n_chunks - 1)
    def _():
        lam = _lambda(lamp_ref, lam_init)
        coef = jnp.where(lax.rem(hc, 2) == 0, 1.0, -lam)
        head_of_lane = lax.broadcasted_iota(jnp.int32, (2 * N_HEADS, ATTN_WIDTH), 1) // V_DIM
        keep = head_of_lane == hc // 2
        o = jnp.where(keep, coef * acc_scr[...] / l_scr[:, :1], 0.0)
        o_ref[0] = jnp.broadcast_to(jnp.sum(o, axis=0, keepdims=True), o_ref.shape[1:])


def _decode_attention(page_table, lamp, q, k_new, v_new, cache_k, cache_v, lam_init, ppc):
    n_req, n_pages = page_table.shape
    assert n_pages % ppc == 0
    n_chunks = n_pages // ppc
    assert cache_k.shape[1:] == (QK_WIDTH, PAGE_SIZE) and cache_v.shape[1:] == (PAGE_ROWS, V_DIM)
    kernel = functools.partial(_decode_kernel, ppc=ppc, n_req=n_req, n_pages=n_pages, lam_init=lam_init)
    const = lambda r, c, pt: (0, 0)
    grid_spec = pltpu.PrefetchScalarGridSpec(
        num_scalar_prefetch=1,
        grid=(n_req, n_chunks),
        in_specs=[
            pl.BlockSpec((4, HEAD_DIM), const),
            pl.BlockSpec((n_req, QK_WIDTH), const),
            pl.BlockSpec((n_req, QK_WIDTH), const),
            pl.BlockSpec((n_req, ATTN_WIDTH), const),
            pl.BlockSpec(memory_space=pl.ANY),
            pl.BlockSpec(memory_space=pl.ANY),
        ],
        out_specs=pl.BlockSpec((1, SUBLANES, ATTN_WIDTH), lambda r, c, pt: (r, 0, 0)),
        scratch_shapes=[
            pltpu.VMEM((2, ppc, QK_WIDTH, PAGE_SIZE), F32),
            pltpu.VMEM((2, ppc * PAGE_ROWS, V_DIM), F32),
            pltpu.SemaphoreType.DMA((2, 2)),
            pltpu.VMEM((2 * N_HEADS, QK_WIDTH), BF16),
            pltpu.VMEM((2 * N_HEADS, LANES), F32),
            pltpu.VMEM((2 * N_HEADS, LANES), F32),
            pltpu.VMEM((2 * N_HEADS, ATTN_WIDTH), F32),
        ],
    )
    out = pl.pallas_call(
        kernel,
        grid_spec=grid_spec,
        out_shape=jax.ShapeDtypeStruct((n_req, SUBLANES, ATTN_WIDTH), F32),
        compiler_params=pltpu.CompilerParams(
            dimension_semantics=("arbitrary", "arbitrary"), vmem_limit_bytes=VMEM_LIMIT_BYTES),
        name="decode_attn",
    )(page_table.reshape(-1), lamp, q, k_new, v_new, cache_k, cache_v)
    return out[:, 0, :]


def _mix_residual(h, att, pooled, gsub_ref, wout_ref, sub_scale):
    parts = [_rms(att[:, hd * V_DIM:(hd + 1) * V_DIM], gsub_ref[...]) * sub_scale for hd in range(N_HEADS)]
    mixed = jnp.concatenate(parts + [pooled], axis=1).astype(BF16)
    return h + _dot(mixed, wout_ref[...])


def _pool_project(d_groups, wpool_ref, pscale_ref):
    ys = [_dot(d.astype(BF16), wpool_ref[g]) for g, d in enumerate(d_groups)]
    return jnp.concatenate(ys, axis=1) * pscale_ref[...]


def _gelu_gate(g, val):
    return 0.5 * g * (1.0 + lax.erf(g * (1.0 / math.sqrt(2.0)))) * val


def _ple(h, p, lnp_ref, wpg_ref, wpp_ref):
    gate = jax.nn.sigmoid(_dot(_rms(h, lnp_ref[...]).astype(BF16), wpg_ref[...]))
    return h + gate * _dot(p.astype(BF16), wpp_ref[...])


def _post_prompt_kernel(h_ref, att_ref, pin_ref, p_ref,
                        gsub_ref, wpool_ref, pscale_ref, wout_ref, ln2_ref,
                        wug_ref, wuv_ref, cw_ref, cb_ref, wdn_ref, lnp_ref, wpg_ref, wpp_ref,
                        y_ref, ctail_ref,
                        pbuf, ubuf_g, ubuf_v, ucarry, a2_scr, act_scr, *, tm, sub_scale):
    t = pl.program_id(1)

    @pl.when(t == 0)
    def _():
        pbuf[0:POOL_HALO, :] = jnp.zeros((POOL_HALO, POOL_WIDTH), F32)
        ucarry[...] = jnp.zeros(ucarry.shape, F32)

    pin = pin_ref[0]
    pbuf[POOL_HALO:, :] = pin
    pos = t * tm + lax.broadcasted_iota(jnp.int32, (tm, 1), 0)
    d_groups = []
    for g, w in enumerate(POOL_WINDOWS):
        gs = slice(g * POOL_GROUP_DIM, (g + 1) * POOL_GROUP_DIM)
        tot = pin[:, gs]
        for back in range(1, w):
            tot = tot + pbuf[POOL_HALO - back:POOL_HALO - back + tm, gs]
        cnt = jnp.minimum(w, pos + 1).astype(F32)
        d_groups.append(tot / cnt - pin[:, gs])
    pbuf[0:POOL_HALO, :] = pbuf[tm:tm + POOL_HALO, :]
    pooled = _pool_project(d_groups, wpool_ref, pscale_ref)

    h1 = _mix_residual(h_ref[0], att_ref[0].astype(F32), pooled, gsub_ref, wout_ref, sub_scale)

    a2_scr[...] = _rms(h1, ln2_ref[...]).astype(BF16)
    for c in range(FF_PAD // FF_CHUNK):
        cs = slice(c * FF_CHUNK, (c + 1) * FF_CHUNK)
        conv = []
        for half, (w_ref, ubuf) in enumerate(((wug_ref, ubuf_g), (wuv_ref, ubuf_v))):
            hs = slice(half * FF_PAD + c * FF_CHUNK, half * FF_PAD + (c + 1) * FF_CHUNK)
            ubuf[0:CONV_HALO, :] = ucarry[:, hs]
            ubuf[CONV_HALO:, :] = _dot(a2_scr[...], w_ref[:, cs])
            acc = cb_ref[:, hs]
            for j in range(CONV_W):
                lo = CONV_HALO - (CONV_W - 1) + j
                acc = acc + ubuf[lo:lo + tm, :] * cw_ref[j:j + 1, hs]
            ucarry[:, hs] = ubuf[tm:tm + CONV_HALO, :]
            conv.append(acc)
        act_scr[:, cs] = _gelu_gate(conv[0], conv[1]).astype(BF16)
    ctail_ref[0] = ucarry[...]
    h2 = h1 + _dot(act_scr[...], wdn_ref[...])

    y_ref[0] = _ple(h2, p_ref[0], lnp_ref, wpg_ref, wpp_ref)


def _post_prompt(h, att, pin, p, weights, sub_scale, tm):
    b, s, _ = h.shape
    assert s % tm == 0
    tile = lambda width: pl.BlockSpec((1, tm, width), lambda bi, t: (bi, t, 0))
    resident = lambda arr: pl.BlockSpec(arr.shape, lambda bi, t: (0,) * arr.ndim,
                                        pipeline_mode=pl.Buffered(1))
    kernel = functools.partial(_post_prompt_kernel, tm=tm, sub_scale=sub_scale)
    return pl.pallas_call(
        kernel,
        grid=(b, s // tm),
        in_specs=[tile(D_MODEL), tile(ATTN_WIDTH), tile(POOL_WIDTH), tile(PLE_DIM)]
        + [resident(w) for w in weights],
        out_specs=(tile(D_MODEL),
                   pl.BlockSpec((1, CONV_HALO, 2 * FF_PAD), lambda bi, t: (bi, 0, 0))),
        out_shape=(jax.ShapeDtypeStruct((b, s, D_MODEL), F32),
                   jax.ShapeDtypeStruct((b, CONV_HALO, 2 * FF_PAD), F32)),
        scratch_shapes=[
            pltpu.VMEM((tm + POOL_HALO, POOL_WIDTH), F32),
            pltpu.VMEM((tm + CONV_HALO, FF_CHUNK), F32),
            pltpu.VMEM((tm + CONV_HALO, FF_CHUNK), F32),
            pltpu.VMEM((CONV_HALO, 2 * FF_PAD), F32),
            pltpu.VMEM((tm, D_MODEL), BF16),
            pltpu.VMEM((tm, FF_PAD), BF16),
        ],
        compiler_params=pltpu.CompilerParams(
            dimension_semantics=("arbitrary", "arbitrary"), vmem_limit_bytes=VMEM_LIMIT_BYTES),
        name="post_prompt",
    )(h, att, pin, p, *weights)


def _post_sample_kernel(h_ref, att_ref, pin_ref, p_ref, spool_ref, sconv_ref,
                        gsub_ref, wpool_ref, pscale_ref, wout_ref, ln2_ref,
                        wug_ref, wuv_ref, cw_ref, cb_ref, wdn_ref, lnp_ref, wpg_ref, wpp_ref,
                        y_ref, u_ref, a2_scr, act_scr, *, sub_scale):
    pin = pin_ref[...]
    d_groups = []
    for g, w in enumerate(POOL_WINDOWS):
        gs = slice(g * POOL_GROUP_DIM, (g + 1) * POOL_GROUP_DIM)
        tot = pin[:, gs]
        for back in range(1, w):
            tot = tot + spool_ref[POOL_BUF - back, :, gs]
        d_groups.append(tot / float(w) - pin[:, gs])
    pooled = _pool_project(d_groups, wpool_ref, pscale_ref)

    h1 = _mix_residual(h_ref[...], att_ref[...], pooled, gsub_ref, wout_ref, sub_scale)

    a2_scr[...] = _rms(h1, ln2_ref[...]).astype(BF16)
    for c in range(FF_PAD // FF_CHUNK):
        cs = slice(c * FF_CHUNK, (c + 1) * FF_CHUNK)
        conv = []
        for half, w_ref in enumerate((wug_ref, wuv_ref)):
            hs = slice(half * FF_PAD + c * FF_CHUNK, half * FF_PAD + (c + 1) * FF_CHUNK)
            u = _dot(a2_scr[...], w_ref[:, cs])
            u_ref[:, hs] = u
            acc = cb_ref[:, hs] + u * cw_ref[CONV_W - 1:CONV_W, hs]
            for j in range(CONV_W - 1):
                acc = acc + sconv_ref[j, :, hs] * cw_ref[j:j + 1, hs]
            conv.append(acc)
        act_scr[:, cs] = _gelu_gate(conv[0], conv[1]).astype(BF16)
    h2 = h1 + _dot(act_scr[...], wdn_ref[...])

    y_ref[...] = _ple(h2, p_ref[...], lnp_ref, wpg_ref, wpp_ref)


def _post_sample(h, att, pin, p, spool, sconv, weights, sub_scale):
    n = h.shape[0]
    args = (h, att, pin, p, spool, sconv) + tuple(weights)
    full = lambda arr: pl.BlockSpec(arr.shape, lambda i: (0,) * arr.ndim, pipeline_mode=pl.Buffered(1))
    kernel = functools.partial(_post_sample_kernel, sub_scale=sub_scale)
    return pl.pallas_call(
        kernel,
        grid=(1,),
        in_specs=[full(a) for a in args],
        out_specs=(pl.BlockSpec((n, D_MODEL), lambda i: (0, 0)),
                   pl.BlockSpec((n, 2 * FF_PAD), lambda i: (0, 0))),
        out_shape=(jax.ShapeDtypeStruct((n, D_MODEL), F32),
                   jax.ShapeDtypeStruct((n, 2 * FF_PAD), F32)),
        scratch_shapes=[pltpu.VMEM((n, D_MODEL), BF16), pltpu.VMEM((n, FF_PAD), BF16)],
        compiler_params=pltpu.CompilerParams(
            dimension_semantics=("arbitrary",), vmem_limit_bytes=VMEM_LIMIT_BYTES),
        name="post_sample",
    )(*args)


def _split_ff(x):
    pad = [(0, 0)] * (x.ndim - 1) + [(0, FF_PAD - D_FF)]
    return jnp.concatenate([jnp.pad(x[..., :D_FF], pad), jnp.pad(x[..., D_FF:], pad)], axis=-1)


def _merge_ff(x):
    return jnp.concatenate([x[..., :D_FF], x[..., FF_PAD:FF_PAD + D_FF]], axis=-1)


def _layer(l, h_p, h_s, cache_k, cache_v, state_pool, state_conv, page_table, p_prompt, p_sample,
           ln1, w_in, g_q, g_k, lam_q1, lam_k1, lam_q2, lam_k2, g_sub, w_pool, pool_scale, w_out,
           ln2, w_up, conv_w, conv_b, w_down, ln_ple, w_pg, w_pp):
    b, s, _ = h_p.shape
    n_req = h_s.shape[0]
    n_pool = cache_k.shape[1]
    lam_init = 0.8 - 0.6 * math.exp(-0.3 * l)
    sub_scale = 1.0 - lam_init

    row = lambda x: x[l].reshape(1, -1)
    win_b = w_in[l].astype(BF16)
    gq = jnp.tile(g_q[l], QK_WIDTH // HEAD_DIM).reshape(1, -1) * (HEAD_DIM ** -0.5)
    gk = jnp.tile(g_k[l], QK_WIDTH // HEAD_DIM).reshape(1, -1)
    grp = jnp.arange(MXU_DIM) // HEAD_DIM
    gmat = jnp.where(grp[:, None] == grp[None, :], 1.0 / HEAD_DIM, 0.0).astype(BF16)
    lamp = jnp.stack([lam_q1[l], lam_k1[l], lam_q2[l], lam_k2[l]])
    wup_b = w_up[l].astype(BF16)
    ffpad = ((0, 0), (0, FF_PAD - D_FF))
    tail_weights = (
        row(g_sub), w_pool[l].astype(BF16), row(pool_scale), w_out[l].astype(BF16), row(ln2),
        jnp.pad(wup_b[:, :D_FF], ffpad), jnp.pad(wup_b[:, D_FF:], ffpad),
        _split_ff(conv_w[l]), _split_ff(conv_b[l].reshape(1, -1)),
        jnp.pad(w_down[l].astype(BF16), ((0, FF_PAD - D_FF), (0, 0))),
        row(ln_ple), w_pg[l].astype(BF16), w_pp[l].astype(BF16),
    )

    q, k, v, pin, kb, vb = _project(h_p.reshape(b * s, D_MODEL), row(ln1), win_b, gmat, gq, gk, tm=256)
    shp = lambda x: x.reshape(b, s, x.shape[-1])
    att = _prompt_attention(lamp, shp(q), shp(kb), shp(vb), lam_init, blk=256)
    y_p, ctail = _post_prompt(h_p, att, shp(pin), p_prompt[l], tail_weights, sub_scale, tm=256)
    k_p = k.reshape(b, s, N_HEADS, 2, HEAD_DIM)
    v_p = v.reshape(b, s, N_HEADS, V_DIM)
    pool_p = shp(pin)[:, s - POOL_BUF:]
    conv_p = _merge_ff(ctail[:, CONV_HALO - (CONV_W - 1):])

    qs, ks, vs, pins, _, _ = _project(h_s.reshape(n_req, D_MODEL), row(ln1), win_b, gmat, gq, gk, tm=n_req)
    ck_view = jnp.transpose(cache_k[l], (0, 2, 3, 4, 1)).reshape(n_pool, QK_WIDTH, PAGE_SIZE)
    cv_view = cache_v[l].reshape(n_pool, PAGE_ROWS, V_DIM)
    att_s = _decode_attention(page_table, lamp, qs.astype(F32), ks, vs, ck_view, cv_view, lam_init, ppc=8)
    sconv = jnp.moveaxis(_split_ff(state_conv[l]), 1, 0)
    y_s, u_s = _post_sample(h_s.reshape(n_req, D_MODEL), att_s, pins, p_sample[l].reshape(n_req, PLE_DIM),
                            jnp.moveaxis(state_pool[l], 1, 0), sconv,
                            tail_weights, sub_scale)
    k_s = ks.reshape(n_req, 1, N_HEADS, 2, HEAD_DIM)
    v_s = vs.reshape(n_req, 1, N_HEADS, V_DIM)
    pool_s = jnp.concatenate([state_pool[l][:, 1:], pins[:, None]], axis=1)
    conv_s = jnp.concatenate([state_conv[l][:, 1:], _merge_ff(u_s)[:, None]], axis=1)

    return (y_p, y_s.reshape(n_req, 1, D_MODEL)), (k_p, v_p, pool_p, conv_p, k_s, v_s, pool_s, conv_s)


def kernel(x_prompt, x_sample, cache_k, cache_v, state_pool, state_conv, page_table, p_prompt, p_sample, ln1, w_in, g_q, g_k, lam_q1, lam_k1, lam_q2, lam_k2, g_sub, w_pool, pool_scale, w_out, ln2, w_up, conv_w, conv_b, w_down, ln_ple, w_pg, w_pp):
    depth = ln1.shape[0]
    h_p, h_s = x_prompt, x_sample
    per_layer = []
    for l in range(depth):
        (h_p, h_s), outs = _layer(
            l, h_p, h_s, cache_k, cache_v, state_pool, state_conv, page_table, p_prompt, p_sample,
            ln1, w_in, g_q, g_k, lam_q1, lam_k1, lam_q2, lam_k2, g_sub, w_pool, pool_scale, w_out,
            ln2, w_up, conv_w, conv_b, w_down, ln_ple, w_pg, w_pp)
        per_layer.append(outs)
    stacked = tuple(jnp.stack(xs) for xs in zip(*per_layer))
    return (h_p, h_s) + stacked
```

```python
import functools
import math

import jax
import jax.numpy as jnp
from jax import lax
from jax.experimental import pallas as pl
from jax.experimental.pallas import tpu as pltpu

F32 = jnp.float32
BF16 = jnp.bfloat16

D_MODEL = 1024
N_HEADS = 4
HEAD_DIM = 64
V_DIM = 2 * HEAD_DIM
QK_WIDTH = N_HEADS * 2 * HEAD_DIM
ATTN_WIDTH = N_HEADS * V_DIM
POOL_WINDOWS = (2, 4, 8, 16)
POOL_WIDTH = D_MODEL - ATTN_WIDTH
POOL_GROUP_DIM = POOL_WIDTH // len(POOL_WINDOWS)
POOL_BUF = max(POOL_WINDOWS) - 1
IN_WIDTH = 2 * QK_WIDTH + ATTN_WIDTH + POOL_WIDTH
D_FF = 2752
CONV_W = 3
PLE_DIM = 256
PAGE_SIZE = 128
PAGE_ROWS = PAGE_SIZE * N_HEADS
EPS = 1e-6
NEG = -1e30

LANES = 128
SUBLANES = 8
MXU_DIM = 256
N_DMA_PRIORITIES = 2
VMEM_LIMIT_BYTES = 56 * 1024 * 1024

FF_PAD = -(-D_FF // MXU_DIM) * MXU_DIM
FF_CHUNK = MXU_DIM
POOL_HALO = 16
CONV_HALO = SUBLANES


def _rms(x, g):
    ms = jnp.mean(x * x, axis=-1, keepdims=True)
    return x * lax.rsqrt(ms + EPS) * g


def _dot(a, b):
    return jnp.dot(a, b, preferred_element_type=F32)


def _dot_nt(a, b):
    return lax.dot_general(a, b, (((1,), (1,)), ((), ())), preferred_element_type=F32)


def _lambda(lamp_ref, lam_init):
    p = lamp_ref[...]
    s1 = jnp.sum(p[0:1] * p[1:2], axis=-1, keepdims=True)
    s2 = jnp.sum(p[2:3] * p[3:4], axis=-1, keepdims=True)
    return jnp.exp(s1) - jnp.exp(s2) + lam_init


def _proj_kernel(h_ref, ln1_ref, win_ref, gmat_ref, gq_ref, gk_ref,
                 q_ref, k_ref, v_ref, pin_ref, kb_ref, vb_ref, a_scr):
    tm = h_ref.shape[0]
    a_scr[...] = _rms(h_ref[...], ln1_ref[...]).astype(BF16)
    gmat = gmat_ref[...]
    n_chunks = IN_WIDTH // MXU_DIM
    per_seg = QK_WIDTH // MXU_DIM
    for c in range(n_chunks):
        z = _dot(a_scr[...], win_ref[:, c * MXU_DIM:(c + 1) * MXU_DIM])
        seg, off = divmod(c, per_seg)
        sl = slice(off * MXU_DIM, (off + 1) * MXU_DIM)
        if seg < 2:
            ms = _dot((z * z).astype(BF16), gmat)
            y = z * lax.rsqrt(ms + EPS)
            if seg == 0:
                q_ref[:, sl] = (y * gq_ref[:, sl]).astype(BF16)
            else:
                y = y * gk_ref[:, sl]
                k_ref[0, sl, :] = y.T
                kb_ref[:, sl] = y.astype(BF16)
        elif seg == 2:
            for hh in range(MXU_DIM // V_DIM):
                head = off * (MXU_DIM // V_DIM) + hh
                v_ref[pl.ds(head, tm, stride=N_HEADS), :] = z[:, hh * V_DIM:(hh + 1) * V_DIM]
            vb_ref[:, sl] = z.astype(BF16)
        else:
            pin_ref[:, sl] = z


def _project(h2d, ln1, win_b, gmat, gq, gk, tm, seq):
    t = h2d.shape[0]
    assert t % seq == 0 and seq % tm == 0
    tiles = seq // tm
    row = lambda i: (i, 0)
    const = lambda i: (0, 0)
    w512 = pl.BlockSpec((tm, QK_WIDTH), row)
    out_shape = (
        jax.ShapeDtypeStruct((t, QK_WIDTH), BF16),
        jax.ShapeDtypeStruct((t // seq, QK_WIDTH, seq), F32),
        jax.ShapeDtypeStruct((t * N_HEADS, V_DIM), F32),
        jax.ShapeDtypeStruct((t, POOL_WIDTH), F32),
        jax.ShapeDtypeStruct((t, QK_WIDTH), BF16),
        jax.ShapeDtypeStruct((t, ATTN_WIDTH), BF16),
    )
    out_specs = (
        w512,
        pl.BlockSpec((1, QK_WIDTH, tm), lambda i: (i // tiles, 0, i % tiles)),
        pl.BlockSpec((tm * N_HEADS, V_DIM), row),
        w512, w512, w512,
    )
    return pl.pallas_call(
        _proj_kernel,
        grid=(t // tm,),
        in_specs=[
            pl.BlockSpec((tm, D_MODEL), row),
            pl.BlockSpec((1, D_MODEL), const),
            pl.BlockSpec((D_MODEL, IN_WIDTH), const, pipeline_mode=pl.Buffered(1)),
            pl.BlockSpec((MXU_DIM, MXU_DIM), const),
            pl.BlockSpec((1, QK_WIDTH), const),
            pl.BlockSpec((1, QK_WIDTH), const),
        ],
        out_specs=out_specs,
        out_shape=out_shape,
        scratch_shapes=[pltpu.VMEM((tm, D_MODEL), BF16)],
        compiler_params=pltpu.CompilerParams(
            dimension_semantics=("arbitrary",), vmem_limit_bytes=VMEM_LIMIT_BYTES),
        name="proj",
    )(h2d, ln1, win_b, gmat, gq, gk)


def _attn_kernel(lamp_ref, q_ref, k_ref, v_ref, o_ref, qq_scr, m_scr, l_scr, acc_scr, *, blk, lam_init):
    i = pl.program_id(1)
    lam = _lambda(lamp_ref, lam_init)
    lane = lax.broadcasted_iota(jnp.int32, (blk, V_DIM), 1)
    row = lax.broadcasted_iota(jnp.int32, (2 * blk, blk), 0)
    col = lax.broadcasted_iota(jnp.int32, (2 * blk, blk), 1)
    causal = col <= jnp.where(row >= blk, row - blk, row)
    reps = blk // LANES
    heads = [slice(h * V_DIM, (h + 1) * V_DIM) for h in range(N_HEADS)]

    for h, hs in enumerate(heads):
        qh = q_ref[0, :, hs]
        zero = jnp.zeros_like(qh)
        qq_scr[h, 0:blk, :] = jnp.where(lane < HEAD_DIM, qh, zero)
        qq_scr[h, blk:2 * blk, :] = jnp.where(lane >= HEAD_DIM, qh, zero)
    m_scr[...] = jnp.full(m_scr.shape, -jnp.inf, F32)
    l_scr[...] = jnp.zeros(l_scr.shape, F32)
    acc_scr[...] = jnp.zeros(acc_scr.shape, F32)

    def step(j, masked):
        start = pl.multiple_of(j * blk, blk)
        for h, hs in enumerate(heads):
            ks = k_ref[0, pl.ds(start, blk), hs]
            vs = v_ref[0, pl.ds(start, blk), hs]
            s = _dot_nt(qq_scr[h], ks)
            if masked:
                s = jnp.where(causal, s, NEG)
            m_prev = m_scr[h]
            m_new = jnp.maximum(m_prev, jnp.max(s, axis=1, keepdims=True))
            alpha = jnp.exp(m_prev - m_new)
            p = jnp.exp(s - jnp.concatenate([m_new] * reps, axis=1))
            l_scr[h] = alpha * l_scr[h] + jnp.sum(p, axis=1, keepdims=True)
            acc_scr[h] = alpha * acc_scr[h] + _dot(p.astype(BF16), vs)
            m_scr[h] = m_new

    def body(j, carry):
        step(j, False)
        return carry

    lax.fori_loop(0, i, body, 0)
    step(i, True)
    for h, hs in enumerate(heads):
        o = acc_scr[h] / l_scr[h]
        o_ref[0, :, hs] = (o[:blk] - lam * o[blk:]).astype(o_ref.dtype)


def _prompt_attention(lamp, q, kb, vb, lam_init, blk):
    b, s, _ = q.shape
    assert s % blk == 0
    kernel = functools.partial(_attn_kernel, blk=blk, lam_init=lam_init)
    return pl.pallas_call(
        kernel,
        grid=(b, s // blk),
        in_specs=[
            pl.BlockSpec((4, HEAD_DIM), lambda bi, i: (0, 0)),
            pl.BlockSpec((1, blk, QK_WIDTH), lambda bi, i: (bi, i, 0)),
            pl.BlockSpec((1, s, QK_WIDTH), lambda bi, i: (bi, 0, 0)),
            pl.BlockSpec((1, s, ATTN_WIDTH), lambda bi, i: (bi, 0, 0)),
        ],
        out_specs=pl.BlockSpec((1, blk, ATTN_WIDTH), lambda bi, i: (bi, i, 0)),
        out_shape=jax.ShapeDtypeStruct((b, s, ATTN_WIDTH), BF16),
        scratch_shapes=[pltpu.VMEM((N_HEADS, 2 * blk, V_DIM), BF16)]
        + [pltpu.VMEM((N_HEADS, 2 * blk, V_DIM), F32)] * 3,
        compiler_params=pltpu.CompilerParams(
            dimension_semantics=("arbitrary", "arbitrary"), vmem_limit_bytes=VMEM_LIMIT_BYTES),
        name="prompt_attn",
    )(lamp, q, kb, vb)


def _decode_kernel(pt_ref, lamp_ref, q_ref, ks_ref, vs_ref, ck_hbm, cv_hbm, o_ref,
                   kbuf, vbuf, sem, qbd_scr, m_scr, l_scr, acc_scr, *, ppc, n_req, n_pages, lam_init):
    r = pl.program_id(0)
    c = pl.program_id(1)
    n_chunks = n_pages // ppc
    step = r * n_chunks + c
    slot = lax.rem(step, 2)

    def page_copies(req, chunk, slot_, page_of):
        copies = []
        for p in range(ppc):
            page = page_of(req * n_pages + chunk * ppc + p)
            dst = pl.ds(p * PAGE_ROWS, PAGE_ROWS)
            copies.append(pltpu.make_async_copy(ck_hbm.at[page], kbuf.at[slot_, p], sem.at[slot_, 0]))
            copies.append(pltpu.make_async_copy(cv_hbm.at[page], vbuf.at[slot_, dst], sem.at[slot_, 1]))
        return copies

    def start_all(copies):
        for n, cp in enumerate(copies):
            cp.start(priority=n % N_DMA_PRIORITIES)

    @pl.when(step == 0)
    def _():
        start_all(page_copies(0, 0, 0, lambda idx: pt_ref[idx]))

    @pl.when(step + 1 < n_req * n_chunks)
    def _():
        nxt = step + 1
        start_all(page_copies(lax.div(nxt, n_chunks), lax.rem(nxt, n_chunks), 1 - slot,
                              lambda idx: pt_ref[idx]))

    hc = lax.broadcasted_iota(jnp.int32, (2 * N_HEADS, QK_WIDTH), 0)
    grp = lax.broadcasted_iota(jnp.int32, (2 * N_HEADS, QK_WIDTH), 1) // HEAD_DIM

    @pl.when(c == 0)
    def _():
        q_row = q_ref[0].astype(F32)
        qbd = jnp.where(hc == grp, jnp.broadcast_to(q_row, hc.shape), 0.0)
        qbd_scr[...] = qbd.astype(BF16)
        s_self = jnp.sum(qbd * ks_ref[0].astype(F32), axis=-1, keepdims=True)
        m_scr[...] = jnp.broadcast_to(s_self, m_scr.shape)
        l_scr[...] = jnp.ones(l_scr.shape, F32)
        acc_scr[...] = jnp.broadcast_to(vs_ref[0].astype(F32), acc_scr.shape)

    for cp in page_copies(0, 0, slot, lambda idx: 0):
        cp.wait()

    qbd = qbd_scr[...]
    s = jnp.concatenate([_dot(qbd, kbuf[slot, p].astype(BF16)) for p in range(ppc)], axis=1)
    m_prev = m_scr[...]
    m_new = jnp.maximum(m_prev, jnp.max(s, axis=1, keepdims=True))
    alpha = jnp.exp(m_prev - m_new)
    p = jnp.exp(s - m_new[:, :1])
    l_scr[...] = alpha * l_scr[...] + jnp.sum(p, axis=1, keepdims=True)
    pb = p.astype(BF16)
    pv = [_dot(pb, vbuf[slot, pl.ds(h, ppc * PAGE_SIZE, stride=N_HEADS), :].astype(BF16))
          for h in range(N_HEADS)]
    acc_scr[...] = alpha[:, :1] * acc_scr[...] + jnp.concatenate(pv, axis=1)
    m_scr[...] = m_new

    @pl.when(c == n_chunks - 1)
    def _():
        lam = _lambda(lamp_ref, lam_init)
        coef = jnp.where(lax.rem(hc, 2) == 0, 1.0, -lam)
        head_of_lane = lax.broadcasted_iota(jnp.int32, (2 * N_HEADS, ATTN_WIDTH), 1) // V_DIM
        keep = head_of_lane == hc // 2
        o = jnp.where(keep, coef * acc_scr[...] / l_scr[:, :1], 0.0)
        o_ref[0] = jnp.broadcast_to(jnp.sum(o, axis=0, keepdims=True), o_ref.shape[1:])


def _decode_attention(page_table, lamp, q, k_new, v_new, cache_k, cache_v, lam_init, ppc):
    n_req, n_pages = page_table.shape
    assert n_pages % ppc == 0
    n_chunks = n_pages // ppc
    assert cache_k.shape[1:] == (QK_WIDTH, PAGE_SIZE) and cache_v.shape[1:] == (PAGE_ROWS, V_DIM)
    kernel = functools.partial(_decode_kernel, ppc=ppc, n_req=n_req, n_pages=n_pages, lam_init=lam_init)
    const = lambda r, c, pt: (0, 0)
    per_req = pl.BlockSpec((1, 1, QK_WIDTH), lambda r, c, pt: (r, 0, 0))
    grid_spec = pltpu.PrefetchScalarGridSpec(
        num_scalar_prefetch=1,
        grid=(n_req, n_chunks),
        in_specs=[
            pl.BlockSpec((4, HEAD_DIM), const),
            per_req, per_req, per_req,
            pl.BlockSpec(memory_space=pl.ANY),
            pl.BlockSpec(memory_space=pl.ANY),
        ],
        out_specs=pl.BlockSpec((1, SUBLANES, ATTN_WIDTH), lambda r, c, pt: (r, 0, 0)),
        scratch_shapes=[
            pltpu.VMEM((2, ppc, QK_WIDTH, PAGE_SIZE), F32),
            pltpu.VMEM((2, ppc * PAGE_ROWS, V_DIM), F32),
            pltpu.SemaphoreType.DMA((2, 2)),
            pltpu.VMEM((2 * N_HEADS, QK_WIDTH), BF16),
            pltpu.VMEM((2 * N_HEADS, LANES), F32),
            pltpu.VMEM((2 * N_HEADS, LANES), F32),
            pltpu.VMEM((2 * N_HEADS, ATTN_WIDTH), F32),
        ],
    )
    out = pl.pallas_call(
        kernel,
        grid_spec=grid_spec,
        out_shape=jax.ShapeDtypeStruct((n_req, SUBLANES, ATTN_WIDTH), F32),
        compiler_params=pltpu.CompilerParams(
            dimension_semantics=("arbitrary", "arbitrary"), vmem_limit_bytes=VMEM_LIMIT_BYTES),
        name="decode_attn",
    )(page_table.reshape(-1), lamp, q[:, None], k_new[:, None], v_new[:, None], cache_k, cache_v)
    return out[:, 0, :]


def _mix_residual(h, att, pooled, gsub_ref, wout_ref, sub_scale):
    parts = [_rms(att[:, hd * V_DIM:(hd + 1) * V_DIM], gsub_ref[...]) * sub_scale for hd in range(N_HEADS)]
    mixed = jnp.concatenate(parts + [pooled], axis=1).astype(BF16)
    return h + _dot(mixed, wout_ref[...])


def _pool_project(d_groups, wpool_ref, pscale_ref):
    ys = [_dot(d.astype(BF16), wpool_ref[g]) for g, d in enumerate(d_groups)]
    return jnp.concatenate(ys, axis=1) * pscale_ref[...]


def _gelu_gate(g, val):
    return 0.5 * g * (1.0 + lax.erf(g * (1.0 / math.sqrt(2.0)))) * val


def _ple(h, p, lnp_ref, wpg_ref, wpp_ref):
    gate = jax.nn.sigmoid(_dot(_rms(h, lnp_ref[...]).astype(BF16), wpg_ref[...]))
    return h + gate * _dot(p.astype(BF16), wpp_ref[...])


def _post_prompt_kernel(h_ref, att_ref, pin_ref, p_ref,
                        gsub_ref, wpool_ref, pscale_ref, wout_ref, ln2_ref,
                        wug_ref, wuv_ref, cw_ref, cb_ref, wdn_ref, lnp_ref, wpg_ref, wpp_ref,
                        y_ref, ctail_ref,
                        pbuf, ubuf_g, ubuf_v, ucarry, a2_scr, act_scr, *, tm, sub_scale):
    t = pl.program_id(1)

    @pl.when(t == 0)
    def _():
        pbuf[0:POOL_HALO, :] = jnp.zeros((POOL_HALO, POOL_WIDTH), F32)
        ucarry[...] = jnp.zeros(ucarry.shape, F32)

    pin = pin_ref[0]
    pbuf[POOL_HALO:, :] = pin
    pos = t * tm + lax.broadcasted_iota(jnp.int32, (tm, 1), 0)
    d_groups = []
    for g, w in enumerate(POOL_WINDOWS):
        gs = slice(g * POOL_GROUP_DIM, (g + 1) * POOL_GROUP_DIM)
        tot = pin[:, gs]
        for back in range(1, w):
            tot = tot + pbuf[POOL_HALO - back:POOL_HALO - back + tm, gs]
        cnt = jnp.minimum(w, pos + 1).astype(F32)
        d_groups.append(tot / cnt - pin[:, gs])
    pbuf[0:POOL_HALO, :] = pbuf[tm:tm + POOL_HALO, :]
    pooled = _pool_project(d_groups, wpool_ref, pscale_ref)

    h1 = _mix_residual(h_ref[0], att_ref[0].astype(F32), pooled, gsub_ref, wout_ref, sub_scale)

    a2_scr[...] = _rms(h1, ln2_ref[...]).astype(BF16)
    for c in range(FF_PAD // FF_CHUNK):
        cs = slice(c * FF_CHUNK, (c + 1) * FF_CHUNK)
        conv = []
        for half, (w_ref, ubuf) in enumerate(((wug_ref, ubuf_g), (wuv_ref, ubuf_v))):
            hs = slice(half * FF_PAD + c * FF_CHUNK, half * FF_PAD + (c + 1) * FF_CHUNK)
            ubuf[0:CONV_HALO, :] = ucarry[:, hs]
            ubuf[CONV_HALO:, :] = _dot(a2_scr[...], w_ref[:, cs])
            acc = cb_ref[:, hs]
            for j in range(CONV_W):
                lo = CONV_HALO - (CONV_W - 1) + j
                acc = acc + ubuf[lo:lo + tm, :] * cw_ref[j:j + 1, hs]
            ucarry[:, hs] = ubuf[tm:tm + CONV_HALO, :]
            conv.append(acc)
        act_scr[:, cs] = _gelu_gate(conv[0], conv[1]).astype(BF16)
    ctail_ref[0] = ucarry[...]
    h2 = h1 + _dot(act_scr[...], wdn_ref[...])

    y_ref[0] = _ple(h2, p_ref[0], lnp_ref, wpg_ref, wpp_ref)


def _post_prompt(h, att, pin, p, weights, sub_scale, tm):
    b, s, _ = h.shape
    assert s % tm == 0
    tile = lambda width: pl.BlockSpec((1, tm, width), lambda bi, t: (bi, t, 0))
    resident = lambda arr: pl.BlockSpec(arr.shape, lambda bi, t: (0,) * arr.ndim,
                                        pipeline_mode=pl.Buffered(1))
    kernel = functools.partial(_post_prompt_kernel, tm=tm, sub_scale=sub_scale)
    return pl.pallas_call(
        kernel,
        grid=(b, s // tm),
        in_specs=[tile(D_MODEL), tile(ATTN_WIDTH), tile(POOL_WIDTH), tile(PLE_DIM)]
        + [resident(w) for w in weights],
        out_specs=(tile(D_MODEL),
                   pl.BlockSpec((1, CONV_HALO, 2 * FF_PAD), lambda bi, t: (bi, 0, 0))),
        out_shape=(jax.ShapeDtypeStruct((b, s, D_MODEL), F32),
                   jax.ShapeDtypeStruct((b, CONV_HALO, 2 * FF_PAD), F32)),
        scratch_shapes=[
            pltpu.VMEM((tm + POOL_HALO, POOL_WIDTH), F32),
            pltpu.VMEM((tm + CONV_HALO, FF_CHUNK), F32),
            pltpu.VMEM((tm + CONV_HALO, FF_CHUNK), F32),
            pltpu.VMEM((CONV_HALO, 2 * FF_PAD), F32),
            pltpu.VMEM((tm, D_MODEL), BF16),
            pltpu.VMEM((tm, FF_PAD), BF16),
        ],
        compiler_params=pltpu.CompilerParams(
            dimension_semantics=("arbitrary", "arbitrary"), vmem_limit_bytes=VMEM_LIMIT_BYTES),
        name="post_prompt",
    )(h, att, pin, p, *weights)


def _post_sample_kernel(h_ref, att_ref, pin_ref, p_ref, spool_ref, sconv_ref,
                        gsub_ref, wpool_ref, pscale_ref, wout_ref, ln2_ref,
                        wug_ref, wuv_ref, cw_ref, cb_ref, wdn_ref, lnp_ref, wpg_ref, wpp_ref,
                        y_ref, u_ref, a2_scr, act_scr, *, sub_scale):
    pin = pin_ref[...]
    d_groups = []
    for g, w in enumerate(POOL_WINDOWS):
        gs = slice(g * POOL_GROUP_DIM, (g + 1) * POOL_GROUP_DIM)
        tot = pin[:, gs]
        for back in range(1, w):
            tot = tot + spool_ref[POOL_BUF - back, :, gs]
        d_groups.append(tot / float(w) - pin[:, gs])
    pooled = _pool_project(d_groups, wpool_ref, pscale_ref)

    h1 = _mix_residual(h_ref[...], att_ref[...], pooled, gsub_ref, wout_ref, sub_scale)

    a2_scr[...] = _rms(h1, ln2_ref[...]).astype(BF16)
    for c in range(FF_PAD // FF_CHUNK):
        cs = slice(c * FF_CHUNK, (c + 1) * FF_CHUNK)
        conv = []
        for half, w_ref in enumerate((wug_ref, wuv_ref)):
            hs = slice(half * FF_PAD + c * FF_CHUNK, half * FF_PAD + (c + 1) * FF_CHUNK)
            u = _dot(a2_scr[...], w_ref[:, cs])
            u_ref[:, hs] = u
            acc = cb_ref[:, hs] + u * cw_ref[CONV_W - 1:CONV_W, hs]
            for j in range(CONV_W - 1):
                acc = acc + sconv_ref[j, :, hs] * cw_ref[j:j + 1, hs]
            conv.append(acc)
        act_scr[:, cs] = _gelu_gate(conv[0], conv[1]).astype(BF16)
    h2 = h1 + _dot(act_scr[...], wdn_ref[...])

    y_ref[...] = _ple(h2, p_ref[...], lnp_ref, wpg_ref, wpp_ref)


def _post_sample(h, att, pin, p, spool, sconv, weights, sub_scale):
    n = h.shape[0]
    args = (h, att, pin, p, spool, sconv) + tuple(weights)
    full = lambda arr: pl.BlockSpec(arr.shape, lambda i: (0,) * arr.ndim, pipeline_mode=pl.Buffered(1))
    kernel = functools.partial(_post_sample_kernel, sub_scale=sub_scale)
    return pl.pallas_call(
        kernel,
        grid=(1,),
        in_specs=[full(a) for a in args],
        out_specs=(pl.BlockSpec((n, D_MODEL), lambda i: (0, 0)),
                   pl.BlockSpec((n, 2 * FF_PAD), lambda i: (0, 0))),
        out_shape=(jax.ShapeDtypeStruct((n, D_MODEL), F32),
                   jax.ShapeDtypeStruct((n, 2 * FF_PAD), F32)),
        scratch_shapes=[pltpu.VMEM((n, D_MODEL), BF16), pltpu.VMEM((n, FF_PAD), BF16)],
        compiler_params=pltpu.CompilerParams(
            dimension_semantics=("arbitrary",), vmem_limit_bytes=VMEM_LIMIT_BYTES),
        name="post_sample",
    )(*args)


def _split_ff(x):
    pad = [(0, 0)] * (x.ndim - 1) + [(0, FF_PAD - D_FF)]
    return jnp.concatenate([jnp.pad(x[..., :D_FF], pad), jnp.pad(x[..., D_FF:], pad)], axis=-1)


def _merge_ff(x):
    return jnp.concatenate([x[..., :D_FF], x[..., FF_PAD:FF_PAD + D_FF]], axis=-1)


def _layer(l, h_p, h_s, cache_k, cache_v, state_pool, state_conv, page_table, p_prompt, p_sample,
           ln1, w_in, g_q, g_k, lam_q1, lam_k1, lam_q2, lam_k2, g_sub, w_pool, pool_scale, w_out,
           ln2, w_up, conv_w, conv_b, w_down, ln_ple, w_pg, w_pp):
    b, s, _ = h_p.shape
    n_req = h_s.shape[0]
    n_pool = cache_k.shape[1]
    lam_init = 0.8 - 0.6 * math.exp(-0.3 * l)
    sub_scale = 1.0 - lam_init

    row = lambda x: x[l].reshape(1, -1)
    win_b = w_in[l].astype(BF16)
    gq = jnp.tile(g_q[l], QK_WIDTH // HEAD_DIM).reshape(1, -1) * (HEAD_DIM ** -0.5)
    gk = jnp.tile(g_k[l], QK_WIDTH // HEAD_DIM).reshape(1, -1)
    grp = jnp.arange(MXU_DIM) // HEAD_DIM
    gmat = jnp.where(grp[:, None] == grp[None, :], 1.0 / HEAD_DIM, 0.0).astype(BF16)
    lamp = jnp.stack([lam_q1[l], lam_k1[l], lam_q2[l], lam_k2[l]])
    wup_b = w_up[l].astype(BF16)
    ffpad = ((0, 0), (0, FF_PAD - D_FF))
    tail_weights = (
        row(g_sub), w_pool[l].astype(BF16), row(pool_scale), w_out[l].astype(BF16), row(ln2),
        jnp.pad(wup_b[:, :D_FF], ffpad), jnp.pad(wup_b[:, D_FF:], ffpad),
        _split_ff(conv_w[l]), _split_ff(conv_b[l].reshape(1, -1)),
        jnp.pad(w_down[l].astype(BF16), ((0, FF_PAD - D_FF), (0, 0))),
        row(ln_ple), w_pg[l].astype(BF16), w_pp[l].astype(BF16),
    )

    q, kt, v4, pin, kb, vb = _project(h_p.reshape(b * s, D_MODEL), row(ln1), win_b, gmat, gq, gk,
                                      tm=256, seq=s)
    shp = lambda x: x.reshape(b, s, x.shape[-1])
    att = _prompt_attention(lamp, shp(q), shp(kb), shp(vb), lam_init, blk=256)
    y_p, ctail = _post_prompt(h_p, att, shp(pin), p_prompt[l], tail_weights, sub_scale, tm=256)
    untranspose_k = lambda x, n: jnp.transpose(x.reshape(-1, N_HEADS, 2, HEAD_DIM, n), (0, 4, 1, 2, 3))
    k_p = untranspose_k(kt, s)
    v_p = v4.reshape(b, s, N_HEADS, V_DIM)
    pool_p = shp(pin)[:, s - POOL_BUF:]
    conv_p = _merge_ff(ctail[:, CONV_HALO - (CONV_W - 1):])

    qs, kts, v4s, pins, kbs, vbs = _project(h_s.reshape(n_req, D_MODEL), row(ln1), win_b, gmat, gq, gk,
                                            tm=n_req, seq=n_req)
    ck_view = jnp.transpose(cache_k[l], (0, 2, 3, 4, 1)).reshape(n_pool, QK_WIDTH, PAGE_SIZE)
    cv_view = cache_v[l].reshape(n_pool, PAGE_ROWS, V_DIM)
    att_s = _decode_attention(page_table, lamp, qs, kbs, vbs, ck_view, cv_view, lam_init, ppc=16)
    sconv = jnp.moveaxis(_split_ff(state_conv[l]), 1, 0)
    y_s, u_s = _post_sample(h_s.reshape(n_req, D_MODEL), att_s, pins, p_sample[l].reshape(n_req, PLE_DIM),
                            jnp.moveaxis(state_pool[l], 1, 0), sconv,
                            tail_weights, sub_scale)
    k_s = untranspose_k(kts, n_req).reshape(n_req, 1, N_HEADS, 2, HEAD_DIM)
    v_s = v4s.reshape(n_req, 1, N_HEADS, V_DIM)
    pool_s = jnp.concatenate([state_pool[l][:, 1:], pins[:, None]], axis=1)
    conv_s = jnp.concatenate([state_conv[l][:, 1:], _merge_ff(u_s)[:, None]], axis=1)

    return (y_p, y_s.reshape(n_req, 1, D_MODEL)), (k_p, v_p, pool_p, conv_p, k_s, v_s, pool_s, conv_s)


def kernel(x_prompt, x_sample, cache_k, cache_v, state_pool, state_conv, page_table, p_prompt, p_sample, ln1, w_in, g_q, g_k, lam_q1, lam_k1, lam_q2, lam_k2, g_sub, w_pool, pool_scale, w_out, ln2, w_up, conv_w, conv_b, w_down, ln_ple, w_pg, w_pp):
    depth = ln1.shape[0]
    h_p, h_s = x_prompt, x_sample
    per_layer = []
    for l in range(depth):
        (h_p, h_s), outs = _layer(
            l, h_p, h_s, cache_k, cache_v, state_pool, state_conv, page_table, p_prompt, p_sample,
            ln1, w_in, g_q, g_k, lam_q1, lam_k1, lam_q2, lam_k2, g_sub, w_pool, pool_scale, w_out,
            ln2, w_up, conv_w, conv_b, w_down, ln_ple, w_pg, w_pp)
        per_layer.append(outs)
    stacked = tuple(jnp.stack(xs) for xs in zip(*per_layer))
    return (h_p, h_s) + stacked
```

```python
import functools
import math

import jax
import jax.numpy as jnp
from jax import lax
from jax.experimental import pallas as pl
from jax.experimental.pallas import tpu as pltpu

F32 = jnp.float32
BF16 = jnp.bfloat16

D_MODEL = 1024
N_HEADS = 4
HEAD_DIM = 64
V_DIM = 2 * HEAD_DIM
QK_WIDTH = N_HEADS * 2 * HEAD_DIM
ATTN_WIDTH = N_HEADS * V_DIM
POOL_WINDOWS = (2, 4, 8, 16)
POOL_WIDTH = D_MODEL - ATTN_WIDTH
POOL_GROUP_DIM = POOL_WIDTH // len(POOL_WINDOWS)
POOL_BUF = max(POOL_WINDOWS) - 1
IN_WIDTH = 2 * QK_WIDTH + ATTN_WIDTH + POOL_WIDTH
D_FF = 2752
CONV_W = 3
PLE_DIM = 256
PAGE_SIZE = 128
PAGE_ROWS = PAGE_SIZE * N_HEADS
EPS = 1e-6
NEG = -1e30

LANES = 128
SUBLANES = 8
MXU_DIM = 256
N_DMA_PRIORITIES = 2
STREAM_SLOTS = 3
STREAM_AHEAD = STREAM_SLOTS - 1
VMEM_LIMIT_BYTES = 56 * 1024 * 1024

FF_PAD = -(-D_FF // MXU_DIM) * MXU_DIM
FF_CHUNK = MXU_DIM
TOKEN_TILE = 256
STREAM_PAGES = 8
POOL_HALO = 16
CONV_HALO = SUBLANES


def _rms(x, g):
    ms = jnp.mean(x * x, axis=-1, keepdims=True)
    return x * lax.rsqrt(ms + EPS) * g


def _dot(a, b):
    return jnp.dot(a, b, preferred_element_type=F32)


def _dot_nt(a, b):
    return lax.dot_general(a, b, (((1,), (1,)), ((), ())), preferred_element_type=F32)


def _lambda(lamp_ref, lam_init):
    p = lamp_ref[...]
    s1 = jnp.sum(p[0:1] * p[1:2], axis=-1, keepdims=True)
    s2 = jnp.sum(p[2:3] * p[3:4], axis=-1, keepdims=True)
    return jnp.exp(s1) - jnp.exp(s2) + lam_init


def _proj_kernel(h_ref, ln1_ref, win_ref, gmat_ref, gq_ref, gk_ref,
                 q_ref, k_ref, v_ref, pin_ref, kb_ref, vb_ref, a_scr):
    tm = h_ref.shape[0]
    a_scr[...] = _rms(h_ref[...], ln1_ref[...]).astype(BF16)
    gmat = gmat_ref[...]
    n_chunks = IN_WIDTH // MXU_DIM
    per_seg = QK_WIDTH // MXU_DIM
    for c in range(n_chunks):
        z = _dot(a_scr[...], win_ref[:, c * MXU_DIM:(c + 1) * MXU_DIM])
        seg, off = divmod(c, per_seg)
        sl = slice(off * MXU_DIM, (off + 1) * MXU_DIM)
        if seg < 2:
            ms = _dot((z * z).astype(BF16), gmat)
            y = z * lax.rsqrt(ms + EPS)
            if seg == 0:
                q_ref[:, sl] = (y * gq_ref[:, sl]).astype(BF16)
            else:
                y = y * gk_ref[:, sl]
                k_ref[0, sl, :] = y.T
                kb_ref[:, sl] = y.astype(BF16)
        elif seg == 2:
            for hh in range(MXU_DIM // V_DIM):
                head = off * (MXU_DIM // V_DIM) + hh
                v_ref[pl.ds(head, tm, stride=N_HEADS), :] = z[:, hh * V_DIM:(hh + 1) * V_DIM]
            vb_ref[:, sl] = z.astype(BF16)
        else:
            pin_ref[:, sl] = z


def _project(h2d, ln1, win_b, gmat, gq, gk, tm, seq):
    t = h2d.shape[0]
    assert t % seq == 0 and seq % tm == 0
    tiles = seq // tm
    row = lambda i: (i, 0)
    const = lambda i: (0, 0)
    w512 = pl.BlockSpec((tm, QK_WIDTH), row)
    out_shape = (
        jax.ShapeDtypeStruct((t, QK_WIDTH), BF16),
        jax.ShapeDtypeStruct((t // seq, QK_WIDTH, seq), F32),
        jax.ShapeDtypeStruct((t * N_HEADS, V_DIM), F32),
        jax.ShapeDtypeStruct((t, POOL_WIDTH), F32),
        jax.ShapeDtypeStruct((t, QK_WIDTH), BF16),
        jax.ShapeDtypeStruct((t, ATTN_WIDTH), BF16),
    )
    out_specs = (
        w512,
        pl.BlockSpec((1, QK_WIDTH, tm), lambda i: (i // tiles, 0, i % tiles)),
        pl.BlockSpec((tm * N_HEADS, V_DIM), row),
        w512, w512, w512,
    )
    return pl.pallas_call(
        _proj_kernel,
        grid=(t // tm,),
        in_specs=[
            pl.BlockSpec((tm, D_MODEL), row),
            pl.BlockSpec((1, D_MODEL), const),
            pl.BlockSpec((D_MODEL, IN_WIDTH), const, pipeline_mode=pl.Buffered(1)),
            pl.BlockSpec((MXU_DIM, MXU_DIM), const),
            pl.BlockSpec((1, QK_WIDTH), const),
            pl.BlockSpec((1, QK_WIDTH), const),
        ],
        out_specs=out_specs,
        out_shape=out_shape,
        scratch_shapes=[pltpu.VMEM((tm, D_MODEL), BF16)],
        compiler_params=pltpu.CompilerParams(
            dimension_semantics=("arbitrary",), vmem_limit_bytes=VMEM_LIMIT_BYTES),
        name="proj",
    )(h2d, ln1, win_b, gmat, gq, gk)


def _stream_scratch(ppc):
    return [
        pltpu.VMEM((STREAM_SLOTS, ppc, QK_WIDTH, PAGE_SIZE), F32),
        pltpu.VMEM((STREAM_SLOTS, ppc * PAGE_ROWS, V_DIM), F32),
        pltpu.SemaphoreType.DMA((STREAM_SLOTS, 2)),
        pltpu.VMEM((2 * N_HEADS, QK_WIDTH), BF16),
        pltpu.VMEM((2 * N_HEADS, LANES), F32),
        pltpu.VMEM((2 * N_HEADS, LANES), F32),
        pltpu.VMEM((2 * N_HEADS, ATTN_WIDTH), F32),
    ]


class _DecodeStream:
    def __init__(self, pt_ref, ck_hbm, cv_hbm, scratch, *, first_req, n_steps, n_pages, ppc):
        self.pt_ref, self.ck_hbm, self.cv_hbm = pt_ref, ck_hbm, cv_hbm
        (self.kbuf, self.vbuf, self.sem, self.qbd, self.m, self.l, self.acc) = scratch
        self.first_req, self.n_pages, self.ppc = first_req, n_pages, ppc
        self.cpr = n_pages // ppc
        self.total = n_steps * self.cpr
        self.hc = lax.broadcasted_iota(jnp.int32, (2 * N_HEADS, QK_WIDTH), 0)

    def _copies(self, g, slot, real):
        base = (self.first_req + lax.div(g, self.cpr)) * self.n_pages + lax.rem(g, self.cpr) * self.ppc
        copies = []
        for p in range(self.ppc):
            page = self.pt_ref[base + p] if real else 0
            rows = pl.ds(p * PAGE_ROWS, PAGE_ROWS)
            copies.append(pltpu.make_async_copy(self.ck_hbm.at[page], self.kbuf.at[slot, p],
                                                self.sem.at[slot, 0]))
            copies.append(pltpu.make_async_copy(self.cv_hbm.at[page], self.vbuf.at[slot, rows],
                                                self.sem.at[slot, 1]))
        return copies

    def _start(self, g):
        for n, cp in enumerate(self._copies(g, lax.rem(g, STREAM_SLOTS), True)):
            cp.start(priority=n % N_DMA_PRIORITIES)

    def prime(self):
        for g in range(STREAM_AHEAD):
            self._start(jnp.int32(g))

    def begin(self, q_row, k_row, v_row):
        grp = lax.broadcasted_iota(jnp.int32, self.hc.shape, 1) // HEAD_DIM
        qbd = jnp.where(self.hc == grp, jnp.broadcast_to(q_row, self.hc.shape), 0.0)
        self.qbd[...] = qbd.astype(BF16)
        s_self = jnp.sum(qbd * k_row, axis=-1, keepdims=True)
        self.m[...] = jnp.broadcast_to(s_self, self.m.shape)
        self.l[...] = jnp.ones(self.l.shape, F32)
        self.acc[...] = jnp.broadcast_to(v_row, self.acc.shape)

    def chunk(self, g):
        ahead = g + STREAM_AHEAD

        @pl.when(ahead < self.total)
        def _():
            self._start(ahead)

        slot = lax.rem(g, STREAM_SLOTS)
        for cp in self._copies(g, slot, False):
            cp.wait()
        n_tok = self.ppc * PAGE_SIZE
        qbd = self.qbd[...]
        s = jnp.concatenate([_dot(qbd, self.kbuf[slot, p].astype(BF16)) for p in range(self.ppc)], axis=1)
        m_prev = self.m[...]
        m_new = jnp.maximum(m_prev, jnp.max(s, axis=1, keepdims=True))
        alpha = jnp.exp(m_prev - m_new)
        p = jnp.exp(s - m_new[:, :1])
        self.l[...] = alpha * self.l[...] + jnp.sum(p, axis=1, keepdims=True)
        pb = p.astype(BF16)
        pv = [_dot(pb, self.vbuf[slot, pl.ds(h, n_tok, stride=N_HEADS), :].astype(BF16))
              for h in range(N_HEADS)]
        self.acc[...] = alpha[:, :1] * self.acc[...] + jnp.concatenate(pv, axis=1)
        self.m[...] = m_new

    def finish(self, lam):
        coef = jnp.where(lax.rem(self.hc, 2) == 0, 1.0, -lam)
        head_of_lane = lax.broadcasted_iota(jnp.int32, self.hc.shape, 1) // V_DIM
        keep = head_of_lane == self.hc // 2
        o = jnp.where(keep, coef * self.acc[...] / self.l[:, :1], 0.0)
        return jnp.sum(o, axis=0, keepdims=True)


def _stream_specs(n_steps_per_b, first_req):
    per_req = pl.BlockSpec((1, 1, QK_WIDTH), lambda bi, i, pt: (first_req + bi * n_steps_per_b + i, 0, 0))
    return [per_req, per_req, per_req, pl.BlockSpec(memory_space=pl.ANY), pl.BlockSpec(memory_space=pl.ANY)]


def _stream_out(n_steps_per_b):
    return (pl.BlockSpec((1, SUBLANES, ATTN_WIDTH), lambda bi, i, pt: (bi * n_steps_per_b + i, 0, 0)),
            lambda n: jax.ShapeDtypeStruct((n, SUBLANES, ATTN_WIDTH), F32))


def _attn_kernel(pt_ref, lamp_ref, q_ref, k_ref, v_ref, qs_ref, ks_ref, vs_ref, ck_hbm, cv_hbm,
                 o_ref, os_ref, qq_scr, m_scr, l_scr, acc_scr, *stream_scr, blk, nq, lam_init, stream_cfg):
    i = pl.program_id(1)
    t = pl.program_id(0) * nq + i
    lam = _lambda(lamp_ref, lam_init)
    stream = _DecodeStream(pt_ref, ck_hbm, cv_hbm, stream_scr, **stream_cfg)

    @pl.when(t == 0)
    def _():
        stream.prime()

    stream.begin(qs_ref[0].astype(F32), ks_ref[0].astype(F32), vs_ref[0].astype(F32))

    lane = lax.broadcasted_iota(jnp.int32, (blk, V_DIM), 1)
    row = lax.broadcasted_iota(jnp.int32, (2 * blk, blk), 0)
    col = lax.broadcasted_iota(jnp.int32, (2 * blk, blk), 1)
    causal = col <= jnp.where(row >= blk, row - blk, row)
    reps = blk // LANES
    heads = [slice(h * V_DIM, (h + 1) * V_DIM) for h in range(N_HEADS)]

    for h, hs in enumerate(heads):
        qh = q_ref[0, :, hs]
        zero = jnp.zeros_like(qh)
        qq_scr[h, 0:blk, :] = jnp.where(lane < HEAD_DIM, qh, zero)
        qq_scr[h, blk:2 * blk, :] = jnp.where(lane >= HEAD_DIM, qh, zero)
    m_scr[...] = jnp.full(m_scr.shape, -jnp.inf, F32)
    l_scr[...] = jnp.zeros(l_scr.shape, F32)
    acc_scr[...] = jnp.zeros(acc_scr.shape, F32)

    def step(j, masked):
        start = pl.multiple_of(j * blk, blk)
        for h, hs in enumerate(heads):
            ks = k_ref[0, pl.ds(start, blk), hs]
            vs = v_ref[0, pl.ds(start, blk), hs]
            s = _dot_nt(qq_scr[h], ks)
            if masked:
                s = jnp.where(causal, s, NEG)
            m_prev = m_scr[h]
            m_new = jnp.maximum(m_prev, jnp.max(s, axis=1, keepdims=True))
            alpha = jnp.exp(m_prev - m_new)
            p = jnp.exp(s - jnp.concatenate([m_new] * reps, axis=1))
            l_scr[h] = alpha * l_scr[h] + jnp.sum(p, axis=1, keepdims=True)
            acc_scr[h] = alpha * acc_scr[h] + _dot(p.astype(BF16), vs)
            m_scr[h] = m_new

    def body(j, carry):
        stream.chunk(t * stream.cpr + j)
        step(j, False)
        return carry

    def rest(c, carry):
        stream.chunk(t * stream.cpr + c)
        return carry

    lax.fori_loop(0, i, body, 0)
    step(i, True)
    lax.fori_loop(i, stream.cpr, rest, 0)
    for h, hs in enumerate(heads):
        o = acc_scr[h] / l_scr[h]
        o_ref[0, :, hs] = (o[:blk] - lam * o[blk:]).astype(o_ref.dtype)
    os_ref[0] = jnp.broadcast_to(stream.finish(lam), os_ref.shape[1:])


def _prompt_attention(page_table, lamp, q, kb, vb, qs, ks_new, vs_new, cache_k, cache_v,
                      lam_init, blk, first_req, ppc):
    b, s, _ = q.shape
    assert s % blk == 0
    nq = s // blk
    n_pages = page_table.shape[1]
    assert n_pages % ppc == 0 and nq <= n_pages // ppc
    stream_cfg = dict(first_req=first_req, n_steps=b * nq, n_pages=n_pages, ppc=ppc)
    kernel = functools.partial(_attn_kernel, blk=blk, nq=nq, lam_init=lam_init, stream_cfg=stream_cfg)
    os_spec, os_shape = _stream_out(nq)
    grid_spec = pltpu.PrefetchScalarGridSpec(
        num_scalar_prefetch=1,
        grid=(b, nq),
        in_specs=[
            pl.BlockSpec((4, HEAD_DIM), lambda bi, i, pt: (0, 0)),
            pl.BlockSpec((1, blk, QK_WIDTH), lambda bi, i, pt: (bi, i, 0)),
            pl.BlockSpec((1, s, QK_WIDTH), lambda bi, i, pt: (bi, 0, 0)),
            pl.BlockSpec((1, s, ATTN_WIDTH), lambda bi, i, pt: (bi, 0, 0)),
        ] + _stream_specs(nq, first_req),
        out_specs=(pl.BlockSpec((1, blk, ATTN_WIDTH), lambda bi, i, pt: (bi, i, 0)), os_spec),
        scratch_shapes=[pltpu.VMEM((N_HEADS, 2 * blk, V_DIM), BF16)]
        + [pltpu.VMEM((N_HEADS, 2 * blk, V_DIM), F32)] * 3 + _stream_scratch(ppc),
    )
    return pl.pallas_call(
        kernel,
        grid_spec=grid_spec,
        out_shape=(jax.ShapeDtypeStruct((b, s, ATTN_WIDTH), BF16), os_shape(b * nq)),
        compiler_params=pltpu.CompilerParams(
            dimension_semantics=("arbitrary", "arbitrary"), vmem_limit_bytes=VMEM_LIMIT_BYTES),
        name="prompt_attn",
    )(page_table.reshape(-1), lamp, q, kb, vb, qs[:, None], ks_new[:, None], vs_new[:, None],
      cache_k, cache_v)


def _mix_residual(h, att, pooled, gsub_ref, wout_ref, sub_scale):
    parts = [_rms(att[:, hd * V_DIM:(hd + 1) * V_DIM], gsub_ref[...]) * sub_scale for hd in range(N_HEADS)]
    mixed = jnp.concatenate(parts + [pooled], axis=1).astype(BF16)
    return h + _dot(mixed, wout_ref[...])


def _pool_project(d_groups, wpool_ref, pscale_ref):
    ys = [_dot(d.astype(BF16), wpool_ref[g]) for g, d in enumerate(d_groups)]
    return jnp.concatenate(ys, axis=1) * pscale_ref[...]


def _gelu_gate(g, val):
    return 0.5 * g * (1.0 + lax.erf(g * (1.0 / math.sqrt(2.0)))) * val


def _ple(h, p, lnp_ref, wpg_ref, wpp_ref):
    gate = jax.nn.sigmoid(_dot(_rms(h, lnp_ref[...]).astype(BF16), wpg_ref[...]))
    return h + gate * _dot(p.astype(BF16), wpp_ref[...])


def _post_prompt_kernel(pt_ref, lamp_ref, h_ref, att_ref, pin_ref, p_ref,
                        qs_ref, ks_ref, vs_ref, ck_hbm, cv_hbm,
                        gsub_ref, wpool_ref, pscale_ref, wout_ref, ln2_ref,
                        wug_ref, wuv_ref, cw_ref, cb_ref, wdn_ref, lnp_ref, wpg_ref, wpp_ref,
                        y_ref, ctail_ref, os_ref,
                        pbuf, ubuf_g, ubuf_v, ucarry, a2_scr, act_scr, *stream_scr,
                        tm, n_tiles, sub_scale, lam_init, stream_cfg):
    t = pl.program_id(1)
    step = pl.program_id(0) * n_tiles + t
    stream = _DecodeStream(pt_ref, ck_hbm, cv_hbm, stream_scr, **stream_cfg)

    @pl.when(step == 0)
    def _():
        stream.prime()

    stream.begin(qs_ref[0].astype(F32), ks_ref[0].astype(F32), vs_ref[0].astype(F32))

    @pl.when(t == 0)
    def _():
        pbuf[0:POOL_HALO, :] = jnp.zeros((POOL_HALO, POOL_WIDTH), F32)
        ucarry[...] = jnp.zeros(ucarry.shape, F32)

    pin = pin_ref[0]
    pbuf[POOL_HALO:, :] = pin
    pos = t * tm + lax.broadcasted_iota(jnp.int32, (tm, 1), 0)
    d_groups = []
    for g, w in enumerate(POOL_WINDOWS):
        gs = slice(g * POOL_GROUP_DIM, (g + 1) * POOL_GROUP_DIM)
        tot = pin[:, gs]
        for back in range(1, w):
            tot = tot + pbuf[POOL_HALO - back:POOL_HALO - back + tm, gs]
        cnt = jnp.minimum(w, pos + 1).astype(F32)
        d_groups.append(tot / cnt - pin[:, gs])
    pbuf[0:POOL_HALO, :] = pbuf[tm:tm + POOL_HALO, :]
    pooled = _pool_project(d_groups, wpool_ref, pscale_ref)

    h1 = _mix_residual(h_ref[0], att_ref[0].astype(F32), pooled, gsub_ref, wout_ref, sub_scale)

    a2_scr[...] = _rms(h1, ln2_ref[...]).astype(BF16)
    for c in range(FF_PAD // FF_CHUNK):
        if c < stream.cpr:
            stream.chunk(step * stream.cpr + c)
        cs = slice(c * FF_CHUNK, (c + 1) * FF_CHUNK)
        conv = []
        for half, (w_ref, ubufs) in enumerate(((wug_ref, ubuf_g), (wuv_ref, ubuf_v))):
            ubuf = ubufs.at[c % 2]
            hs = slice(half * FF_PAD + c * FF_CHUNK, half * FF_PAD + (c + 1) * FF_CHUNK)
            ubuf[0:CONV_HALO, :] = ucarry[:, hs]
            ubuf[CONV_HALO:, :] = _dot(a2_scr[...], w_ref[:, cs])
            acc = cb_ref[:, hs]
            for j in range(CONV_W):
                lo = CONV_HALO - (CONV_W - 1) + j
                acc = acc + ubuf[lo:lo + tm, :] * cw_ref[j:j + 1, hs]
            ucarry[:, hs] = ubuf[tm:tm + CONV_HALO, :]
            conv.append(acc)
        act_scr[:, cs] = _gelu_gate(conv[0], conv[1]).astype(BF16)
    ctail_ref[0] = ucarry[...]
    h2 = h1 + _dot(act_scr[...], wdn_ref[...])

    y_ref[0] = _ple(h2, p_ref[0], lnp_ref, wpg_ref, wpp_ref)
    os_ref[0] = jnp.broadcast_to(stream.finish(_lambda(lamp_ref, lam_init)), os_ref.shape[1:])


def _post_prompt(page_table, lamp, h, att, pin, p, qs, ks_new, vs_new, cache_k, cache_v, weights,
                 sub_scale, lam_init, tm, first_req, ppc):
    b, s, _ = h.shape
    assert s % tm == 0
    n_tiles = s // tm
    n_pages = page_table.shape[1]
    assert n_pages % ppc == 0 and n_pages // ppc <= FF_PAD // FF_CHUNK
    tile = lambda width: pl.BlockSpec((1, tm, width), lambda bi, t, pt: (bi, t, 0))
    resident = lambda arr: pl.BlockSpec(arr.shape, lambda bi, t, pt: (0,) * arr.ndim,
                                        pipeline_mode=pl.Buffered(1))
    stream_cfg = dict(first_req=first_req, n_steps=b * n_tiles, n_pages=n_pages, ppc=ppc)
    kernel = functools.partial(_post_prompt_kernel, tm=tm, n_tiles=n_tiles, sub_scale=sub_scale,
                               lam_init=lam_init, stream_cfg=stream_cfg)
    os_spec, os_shape = _stream_out(n_tiles)
    grid_spec = pltpu.PrefetchScalarGridSpec(
        num_scalar_prefetch=1,
        grid=(b, n_tiles),
        in_specs=[pl.BlockSpec((4, HEAD_DIM), lambda bi, t, pt: (0, 0)),
                  tile(D_MODEL), tile(ATTN_WIDTH), tile(POOL_WIDTH), tile(PLE_DIM)]
        + _stream_specs(n_tiles, first_req) + [resident(w) for w in weights],
        out_specs=(tile(D_MODEL),
                   pl.BlockSpec((1, CONV_HALO, 2 * FF_PAD), lambda bi, t, pt: (bi, 0, 0)),
                   os_spec),
        scratch_shapes=[
            pltpu.VMEM((tm + POOL_HALO, POOL_WIDTH), F32),
            pltpu.VMEM((2, tm + CONV_HALO, FF_CHUNK), F32),
            pltpu.VMEM((2, tm + CONV_HALO, FF_CHUNK), F32),
            pltpu.VMEM((CONV_HALO, 2 * FF_PAD), F32),
            pltpu.VMEM((tm, D_MODEL), BF16),
            pltpu.VMEM((tm, FF_PAD), BF16),
        ] + _stream_scratch(ppc),
    )
    return pl.pallas_call(
        kernel,
        grid_spec=grid_spec,
        out_shape=(jax.ShapeDtypeStruct((b, s, D_MODEL), F32),
                   jax.ShapeDtypeStruct((b, CONV_HALO, 2 * FF_PAD), F32),
                   os_shape(b * n_tiles)),
        compiler_params=pltpu.CompilerParams(
            dimension_semantics=("arbitrary", "arbitrary"), vmem_limit_bytes=VMEM_LIMIT_BYTES),
        name="post_prompt",
    )(page_table.reshape(-1), lamp, h, att, pin, p, qs[:, None], ks_new[:, None], vs_new[:, None],
      cache_k, cache_v, *weights)


def _post_sample_kernel(h_ref, att_ref, pin_ref, p_ref, spool_ref, sconv_ref,
                        gsub_ref, wpool_ref, pscale_ref, wout_ref, ln2_ref,
                        wug_ref, wuv_ref, cw_ref, cb_ref, wdn_ref, lnp_ref, wpg_ref, wpp_ref,
                        y_ref, u_ref, a2_scr, act_scr, *, sub_scale):
    pin = pin_ref[...]
    d_groups = []
    for g, w in enumerate(POOL_WINDOWS):
        gs = slice(g * POOL_GROUP_DIM, (g + 1) * POOL_GROUP_DIM)
        tot = pin[:, gs]
        for back in range(1, w):
            tot = tot + spool_ref[POOL_BUF - back, :, gs]
        d_groups.append(tot / float(w) - pin[:, gs])
    pooled = _pool_project(d_groups, wpool_ref, pscale_ref)

    h1 = _mix_residual(h_ref[...], att_ref[...], pooled, gsub_ref, wout_ref, sub_scale)

    a2_scr[...] = _rms(h1, ln2_ref[...]).astype(BF16)
    for c in range(FF_PAD // FF_CHUNK):
        cs = slice(c * FF_CHUNK, (c + 1) * FF_CHUNK)
        conv = []
        for half, w_ref in enumerate((wug_ref, wuv_ref)):
            hs = slice(half * FF_PAD + c * FF_CHUNK, half * FF_PAD + (c + 1) * FF_CHUNK)
            u = _dot(a2_scr[...], w_ref[:, cs])
            u_ref[:, hs] = u
            acc = cb_ref[:, hs] + u * cw_ref[CONV_W - 1:CONV_W, hs]
            for j in range(CONV_W - 1):
                acc = acc + sconv_ref[j, :, hs] * cw_ref[j:j + 1, hs]
            conv.append(acc)
        act_scr[:, cs] = _gelu_gate(conv[0], conv[1]).astype(BF16)
    h2 = h1 + _dot(act_scr[...], wdn_ref[...])

    y_ref[...] = _ple(h2, p_ref[...], lnp_ref, wpg_ref, wpp_ref)


def _post_sample(h, att, pin, p, spool, sconv, weights, sub_scale):
    n = h.shape[0]
    args = (h, att, pin, p, spool, sconv) + tuple(weights)
    full = lambda arr: pl.BlockSpec(arr.shape, lambda i: (0,) * arr.ndim, pipeline_mode=pl.Buffered(1))
    kernel = functools.partial(_post_sample_kernel, sub_scale=sub_scale)
    return pl.pallas_call(
        kernel,
        grid=(1,),
        in_specs=[full(a) for a in args],
        out_specs=(pl.BlockSpec((n, D_MODEL), lambda i: (0, 0)),
                   pl.BlockSpec((n, 2 * FF_PAD), lambda i: (0, 0))),
        out_shape=(jax.ShapeDtypeStruct((n, D_MODEL), F32),
                   jax.ShapeDtypeStruct((n, 2 * FF_PAD), F32)),
        scratch_shapes=[pltpu.VMEM((n, D_MODEL), BF16), pltpu.VMEM((n, FF_PAD), BF16)],
        compiler_params=pltpu.CompilerParams(
            dimension_semantics=("arbitrary",), vmem_limit_bytes=VMEM_LIMIT_BYTES),
        name="post_sample",
    )(*args)


def _split_ff(x):
    pad = [(0, 0)] * (x.ndim - 1) + [(0, FF_PAD - D_FF)]
    return jnp.concatenate([jnp.pad(x[..., :D_FF], pad), jnp.pad(x[..., D_FF:], pad)], axis=-1)


def _merge_ff(x):
    return jnp.concatenate([x[..., :D_FF], x[..., FF_PAD:FF_PAD + D_FF]], axis=-1)


def _layer(l, h_p, h_s, cache_k, cache_v, state_pool, state_conv, page_table, p_prompt, p_sample,
           ln1, w_in, g_q, g_k, lam_q1, lam_k1, lam_q2, lam_k2, g_sub, w_pool, pool_scale, w_out,
           ln2, w_up, conv_w, conv_b, w_down, ln_ple, w_pg, w_pp):
    b, s, _ = h_p.shape
    n_req = h_s.shape[0]
    n_pool = cache_k.shape[1]
    lam_init = 0.8 - 0.6 * math.exp(-0.3 * l)
    sub_scale = 1.0 - lam_init

    row = lambda x: x[l].reshape(1, -1)
    win_b = w_in[l].astype(BF16)
    gq = jnp.tile(g_q[l], QK_WIDTH // HEAD_DIM).reshape(1, -1) * (HEAD_DIM ** -0.5)
    gk = jnp.tile(g_k[l], QK_WIDTH // HEAD_DIM).reshape(1, -1)
    grp = jnp.arange(MXU_DIM) // HEAD_DIM
    gmat = jnp.where(grp[:, None] == grp[None, :], 1.0 / HEAD_DIM, 0.0).astype(BF16)
    lamp = jnp.stack([lam_q1[l], lam_k1[l], lam_q2[l], lam_k2[l]])
    wup_b = w_up[l].astype(BF16)
    ffpad = ((0, 0), (0, FF_PAD - D_FF))
    tail_weights = (
        row(g_sub), w_pool[l].astype(BF16), row(pool_scale), w_out[l].astype(BF16), row(ln2),
        jnp.pad(wup_b[:, :D_FF], ffpad), jnp.pad(wup_b[:, D_FF:], ffpad),
        _split_ff(conv_w[l]), _split_ff(conv_b[l].reshape(1, -1)),
        jnp.pad(w_down[l].astype(BF16), ((0, FF_PAD - D_FF), (0, 0))),
        row(ln_ple), w_pg[l].astype(BF16), w_pp[l].astype(BF16),
    )

    qs, kts, v4s, pins, kbs, vbs = _project(h_s.reshape(n_req, D_MODEL), row(ln1), win_b, gmat, gq, gk,
                                            tm=n_req, seq=n_req)
    ck_view = jnp.transpose(cache_k[l], (0, 2, 3, 4, 1)).reshape(n_pool, QK_WIDTH, PAGE_SIZE)
    cv_view = cache_v[l].reshape(n_pool, PAGE_ROWS, V_DIM)
    stream_args = (qs, kbs, vbs, ck_view, cv_view)

    q, kt, v4, pin, kb, vb = _project(h_p.reshape(b * s, D_MODEL), row(ln1), win_b, gmat, gq, gk,
                                      tm=TOKEN_TILE, seq=s)
    shp = lambda x: x.reshape(b, s, x.shape[-1])
    n_attn_req = b * (s // TOKEN_TILE)
    assert n_req == 2 * n_attn_req, "one decode request per grid step of the two host kernels"
    att, att_s0 = _prompt_attention(page_table, lamp, shp(q), shp(kb), shp(vb), *stream_args,
                                    lam_init, blk=TOKEN_TILE, first_req=0, ppc=STREAM_PAGES)
    y_p, ctail, att_s1 = _post_prompt(page_table, lamp, h_p, att, shp(pin), p_prompt[l], *stream_args,
                                      tail_weights, sub_scale, lam_init, tm=TOKEN_TILE,
                                      first_req=n_attn_req, ppc=STREAM_PAGES)
    untranspose_k = lambda x, n: jnp.transpose(x.reshape(-1, N_HEADS, 2, HEAD_DIM, n), (0, 4, 1, 2, 3))
    k_p = untranspose_k(kt, s)
    v_p = v4.reshape(b, s, N_HEADS, V_DIM)
    pool_p = shp(pin)[:, s - POOL_BUF:]
    conv_p = _merge_ff(ctail[:, CONV_HALO - (CONV_W - 1):])

    att_s = jnp.concatenate([att_s0[:, 0], att_s1[:, 0]], axis=0)
    sconv = jnp.moveaxis(_split_ff(state_conv[l]), 1, 0)
    y_s, u_s = _post_sample(h_s.reshape(n_req, D_MODEL), att_s, pins, p_sample[l].reshape(n_req, PLE_DIM),
                            jnp.moveaxis(state_pool[l], 1, 0), sconv,
                            tail_weights, sub_scale)
    k_s = untranspose_k(kts, n_req).reshape(n_req, 1, N_HEADS, 2, HEAD_DIM)
    v_s = v4s.reshape(n_req, 1, N_HEADS, V_DIM)
    pool_s = jnp.concatenate([state_pool[l][:, 1:], pins[:, None]], axis=1)
    conv_s = jnp.concatenate([state_conv[l][:, 1:], _merge_ff(u_s)[:, None]], axis=1)

    return (y_p, y_s.reshape(n_req, 1, D_MODEL)), (k_p, v_p, pool_p, conv_p, k_s, v_s, pool_s, conv_s)


def kernel(x_prompt, x_sample, cache_k, cache_v, state_pool, state_conv, page_table, p_prompt, p_sample, ln1, w_in, g_q, g_k, lam_q1, lam_k1, lam_q2, lam_k2, g_sub, w_pool, pool_scale, w_out, ln2, w_up, conv_w, conv_b, w_down, ln_ple, w_pg, w_pp):
    depth = ln1.shape[0]
    h_p, h_s = x_prompt, x_sample
    per_layer = []
    for l in range(depth):
        (h_p, h_s), outs = _layer(
            l, h_p, h_s, cache_k, cache_v, state_pool, state_conv, page_table, p_prompt, p_sample,
            ln1, w_in, g_q, g_k, lam_q1, lam_k1, lam_q2, lam_k2, g_sub, w_pool, pool_scale, w_out,
            ln2, w_up, conv_w, conv_b, w_down, ln_ple, w_pg, w_pp)
        per_layer.append(outs)
    stacked = tuple(jnp.stack(xs) for xs in zip(*per_layer))
    return (h_p, h_s) + stacked
```

```python
import functools
import math

import jax
import jax.numpy as jnp
from jax import lax
from jax.experimental import pallas as pl
from jax.experimental.pallas import tpu as pltpu

F32 = jnp.float32
BF16 = jnp.bfloat16

D_MODEL = 1024
N_HEADS = 4
HEAD_DIM = 64
V_DIM = 2 * HEAD_DIM
QK_WIDTH = N_HEADS * 2 * HEAD_DIM
ATTN_WIDTH = N_HEADS * V_DIM
POOL_WINDOWS = (2, 4, 8, 16)
POOL_WIDTH = D_MODEL - ATTN_WIDTH
POOL_GROUP_DIM = POOL_WIDTH // len(POOL_WINDOWS)
POOL_BUF = max(POOL_WINDOWS) - 1
IN_WIDTH = 2 * QK_WIDTH + ATTN_WIDTH + POOL_WIDTH
D_FF = 2752
CONV_W = 3
PLE_DIM = 256
PAGE_SIZE = 128
PAGE_ROWS = PAGE_SIZE * N_HEADS
EPS = 1e-6
NEG = -1e30

LANES = 128
SUBLANES = 8
MXU_DIM = 256
N_DMA_PRIORITIES = 2
STREAM_AHEAD = 2
STREAM_SLOTS = STREAM_AHEAD + 2
VMEM_LIMIT_BYTES = 56 * 1024 * 1024

FF_PAD = -(-D_FF // MXU_DIM) * MXU_DIM
FF_CHUNK = MXU_DIM
TOKEN_TILE = 256
PROJ_TILE = 512
STREAM_PAGES = 8
POOL_HALO = 16
CONV_HALO = SUBLANES


def _rms(x, g):
    ms = jnp.mean(x * x, axis=-1, keepdims=True)
    return x * lax.rsqrt(ms + EPS) * g


def _dot(a, b):
    return jnp.dot(a, b, preferred_element_type=F32)


def _dot_nt(a, b):
    return lax.dot_general(a, b, (((1,), (1,)), ((), ())), preferred_element_type=F32)


def _lambda(lamp_ref, lam_init):
    p = lamp_ref[...]
    s1 = jnp.sum(p[0:1] * p[1:2], axis=-1, keepdims=True)
    s2 = jnp.sum(p[2:3] * p[3:4], axis=-1, keepdims=True)
    return jnp.exp(s1) - jnp.exp(s2) + lam_init


def _proj_kernel(h_ref, ln1_ref, win_ref, gmat_ref, gq_ref, gk_ref,
                 q_ref, k_ref, v_ref, pin_ref, kb_ref, vb_ref, a_scr):
    tm = h_ref.shape[0]
    a_scr[...] = _rms(h_ref[...], ln1_ref[...]).astype(BF16)
    gmat = gmat_ref[...]
    n_chunks = IN_WIDTH // MXU_DIM
    per_seg = QK_WIDTH // MXU_DIM
    for c in range(n_chunks):
        z = _dot(a_scr[...], win_ref[:, c * MXU_DIM:(c + 1) * MXU_DIM])
        seg, off = divmod(c, per_seg)
        sl = slice(off * MXU_DIM, (off + 1) * MXU_DIM)
        if seg < 2:
            ms = _dot((z * z).astype(BF16), gmat)
            y = z * lax.rsqrt(ms + EPS)
            if seg == 0:
                q_ref[:, sl] = (y * gq_ref[:, sl]).astype(BF16)
            else:
                y = y * gk_ref[:, sl]
                k_ref[0, sl, :] = y.T
                kb_ref[:, sl] = y.astype(BF16)
        elif seg == 2:
            for hh in range(MXU_DIM // V_DIM):
                head = off * (MXU_DIM // V_DIM) + hh
                v_ref[pl.ds(head, tm, stride=N_HEADS), :] = z[:, hh * V_DIM:(hh + 1) * V_DIM]
            vb_ref[:, sl] = z.astype(BF16)
        else:
            pin_ref[:, sl] = z


def _project(h2d, ln1, win_b, gmat, gq, gk, tm, seq):
    t = h2d.shape[0]
    assert t % seq == 0 and seq % tm == 0
    tiles = seq // tm
    row = lambda i: (i, 0)
    const = lambda i: (0, 0)
    w512 = pl.BlockSpec((tm, QK_WIDTH), row)
    out_shape = (
        jax.ShapeDtypeStruct((t, QK_WIDTH), BF16),
        jax.ShapeDtypeStruct((t // seq, QK_WIDTH, seq), F32),
        jax.ShapeDtypeStruct((t * N_HEADS, V_DIM), F32),
        jax.ShapeDtypeStruct((t, POOL_WIDTH), F32),
        jax.ShapeDtypeStruct((t, QK_WIDTH), BF16),
        jax.ShapeDtypeStruct((t, ATTN_WIDTH), BF16),
    )
    out_specs = (
        w512,
        pl.BlockSpec((1, QK_WIDTH, tm), lambda i: (i // tiles, 0, i % tiles)),
        pl.BlockSpec((tm * N_HEADS, V_DIM), row),
        w512, w512, w512,
    )
    return pl.pallas_call(
        _proj_kernel,
        grid=(t // tm,),
        in_specs=[
            pl.BlockSpec((tm, D_MODEL), row),
            pl.BlockSpec((1, D_MODEL), const),
            pl.BlockSpec((D_MODEL, IN_WIDTH), const, pipeline_mode=pl.Buffered(1)),
            pl.BlockSpec((MXU_DIM, MXU_DIM), const),
            pl.BlockSpec((1, QK_WIDTH), const),
            pl.BlockSpec((1, QK_WIDTH), const),
        ],
        out_specs=out_specs,
        out_shape=out_shape,
        scratch_shapes=[pltpu.VMEM((tm, D_MODEL), BF16)],
        compiler_params=pltpu.CompilerParams(
            dimension_semantics=("arbitrary",), vmem_limit_bytes=VMEM_LIMIT_BYTES),
        name="proj",
    )(h2d, ln1, win_b, gmat, gq, gk)


def _stream_scratch(ppc):
    return [
        pltpu.VMEM((STREAM_SLOTS, ppc, QK_WIDTH, PAGE_SIZE), F32),
        pltpu.VMEM((STREAM_SLOTS, ppc * PAGE_ROWS, V_DIM), F32),
        pltpu.SemaphoreType.DMA((STREAM_SLOTS, 2)),
        pltpu.VMEM((2 * N_HEADS, QK_WIDTH), BF16),
        pltpu.VMEM((2 * N_HEADS, LANES), F32),
        pltpu.VMEM((2 * N_HEADS, LANES), F32),
        pltpu.VMEM((2 * N_HEADS, ATTN_WIDTH), F32),
    ]


class _DecodeStream:
    def __init__(self, pt_ref, ck_hbm, cv_hbm, scratch, *, first_req, n_steps, n_pages, ppc):
        self.pt_ref, self.ck_hbm, self.cv_hbm = pt_ref, ck_hbm, cv_hbm
        (self.kbuf, self.vbuf, self.sem, self.qbd, self.m, self.l, self.acc) = scratch
        self.first_req, self.n_pages, self.ppc = first_req, n_pages, ppc
        self.cpr = n_pages // ppc
        self.total = n_steps * self.cpr
        self.hc = lax.broadcasted_iota(jnp.int32, (2 * N_HEADS, QK_WIDTH), 0)

    def _copies(self, g, slot, real):
        base = (self.first_req + lax.div(g, self.cpr)) * self.n_pages + lax.rem(g, self.cpr) * self.ppc
        copies = []
        for p in range(self.ppc):
            page = self.pt_ref[base + p] if real else 0
            rows = pl.ds(p * PAGE_ROWS, PAGE_ROWS)
            copies.append(pltpu.make_async_copy(self.ck_hbm.at[page], self.kbuf.at[slot, p],
                                                self.sem.at[slot, 0]))
            copies.append(pltpu.make_async_copy(self.cv_hbm.at[page], self.vbuf.at[slot, rows],
                                                self.sem.at[slot, 1]))
        return copies

    def _start(self, g):
        for n, cp in enumerate(self._copies(g, lax.rem(g, STREAM_SLOTS), True)):
            cp.start(priority=n % N_DMA_PRIORITIES)

    def prime(self):
        for g in range(STREAM_AHEAD):
            self._start(jnp.int32(g))

    def begin(self, q_row, k_row, v_row):
        grp = lax.broadcasted_iota(jnp.int32, self.hc.shape, 1) // HEAD_DIM
        qbd = jnp.where(self.hc == grp, jnp.broadcast_to(q_row, self.hc.shape), 0.0)
        self.qbd[...] = qbd.astype(BF16)
        s_self = jnp.sum(qbd * k_row, axis=-1, keepdims=True)
        self.m[...] = jnp.broadcast_to(s_self, self.m.shape)
        self.l[...] = jnp.ones(self.l.shape, F32)
        self.acc[...] = jnp.broadcast_to(v_row, self.acc.shape)

    def fetch(self, g):
        ahead = g + STREAM_AHEAD

        @pl.when(ahead < self.total)
        def _():
            self._start(ahead)

        for cp in self._copies(g, lax.rem(g, STREAM_SLOTS), False):
            cp.wait()

    def scores(self, g):
        slot = lax.rem(g, STREAM_SLOTS)
        qbd = self.qbd[...]
        return jnp.concatenate(
            [_dot(qbd, self.kbuf[slot, p].astype(BF16)) for p in range(self.ppc)], axis=1)

    def update(self, g, s):
        slot = lax.rem(g, STREAM_SLOTS)
        n_tok = self.ppc * PAGE_SIZE
        m_prev = self.m[...]
        m_new = jnp.maximum(m_prev, jnp.max(s, axis=1, keepdims=True))
        alpha = jnp.exp(m_prev - m_new)
        p = jnp.exp(s - m_new[:, :1])
        self.l[...] = alpha * self.l[...] + jnp.sum(p, axis=1, keepdims=True)
        pb = p.astype(BF16)
        pv = [_dot(pb, self.vbuf[slot, pl.ds(h, n_tok, stride=N_HEADS), :].astype(BF16))
              for h in range(N_HEADS)]
        self.acc[...] = alpha[:, :1] * self.acc[...] + jnp.concatenate(pv, axis=1)
        self.m[...] = m_new

    def finish(self, lam):
        coef = jnp.where(lax.rem(self.hc, 2) == 0, 1.0, -lam)
        head_of_lane = lax.broadcasted_iota(jnp.int32, self.hc.shape, 1) // V_DIM
        keep = head_of_lane == self.hc // 2
        o = jnp.where(keep, coef * self.acc[...] / self.l[:, :1], 0.0)
        return jnp.sum(o, axis=0, keepdims=True)


def _stream_specs(n_steps_per_b, first_req):
    per_req = pl.BlockSpec((1, 1, QK_WIDTH), lambda bi, i, pt: (first_req + bi * n_steps_per_b + i, 0, 0))
    return [per_req, per_req, per_req, pl.BlockSpec(memory_space=pl.ANY), pl.BlockSpec(memory_space=pl.ANY)]


def _stream_out(n_steps_per_b):
    return (pl.BlockSpec((1, SUBLANES, ATTN_WIDTH), lambda bi, i, pt: (bi * n_steps_per_b + i, 0, 0)),
            lambda n: jax.ShapeDtypeStruct((n, SUBLANES, ATTN_WIDTH), F32))


def _attn_kernel(pt_ref, lamp_ref, q_ref, k_ref, v_ref, qs_ref, ks_ref, vs_ref, ck_hbm, cv_hbm,
                 o_ref, os_ref, qq_scr, m_scr, l_scr, acc_scr, *stream_scr, blk, nq, lam_init, stream_cfg):
    i = pl.program_id(1)
    t = pl.program_id(0) * nq + i
    lam = _lambda(lamp_ref, lam_init)
    stream = _DecodeStream(pt_ref, ck_hbm, cv_hbm, stream_scr, **stream_cfg)

    @pl.when(t == 0)
    def _():
        stream.prime()

    stream.begin(qs_ref[0].astype(F32), ks_ref[0].astype(F32), vs_ref[0].astype(F32))

    lane = lax.broadcasted_iota(jnp.int32, (blk, V_DIM), 1)
    row = lax.broadcasted_iota(jnp.int32, (2 * blk, blk), 0)
    col = lax.broadcasted_iota(jnp.int32, (2 * blk, blk), 1)
    causal = col <= jnp.where(row >= blk, row - blk, row)
    reps = blk // LANES
    heads = [slice(h * V_DIM, (h + 1) * V_DIM) for h in range(N_HEADS)]

    for h, hs in enumerate(heads):
        qh = q_ref[0, :, hs]
        zero = jnp.zeros_like(qh)
        qq_scr[h, 0:blk, :] = jnp.where(lane < HEAD_DIM, qh, zero)
        qq_scr[h, blk:2 * blk, :] = jnp.where(lane >= HEAD_DIM, qh, zero)
    m_scr[...] = jnp.full(m_scr.shape, -jnp.inf, F32)
    l_scr[...] = jnp.zeros(l_scr.shape, F32)
    acc_scr[...] = jnp.zeros(acc_scr.shape, F32)

    def attn_scores(j, masked, which):
        start = pl.multiple_of(j * blk, blk)
        out = []
        for h in which:
            s = _dot_nt(qq_scr[h], k_ref[0, pl.ds(start, blk), heads[h]])
            out.append(jnp.where(causal, s, NEG) if masked else s)
        return out

    def attn_update(j, scores, which):
        start = pl.multiple_of(j * blk, blk)
        for s, h in zip(scores, which):
            hs = heads[h]
            m_prev = m_scr[h]
            m_new = jnp.maximum(m_prev, jnp.max(s, axis=1, keepdims=True))
            alpha = jnp.exp(m_prev - m_new)
            p = jnp.exp(s - jnp.concatenate([m_new] * reps, axis=1))
            l_scr[h] = alpha * l_scr[h] + jnp.sum(p, axis=1, keepdims=True)
            acc_scr[h] = alpha * acc_scr[h] + _dot(p.astype(BF16), v_ref[0, pl.ds(start, blk), hs])
            m_scr[h] = m_new

    g0 = t * stream.cpr
    stream.fetch(g0)
    s0 = stream.scores(g0)

    def body(j, s):
        stream.fetch(g0 + j + 1)
        half = N_HEADS // 2
        first = attn_scores(j, False, range(half))
        s_next = stream.scores(g0 + j + 1)
        attn_update(j, first, range(half))
        second = attn_scores(j, False, range(half, N_HEADS))
        stream.update(g0 + j, s)
        attn_update(j, second, range(half, N_HEADS))
        return s_next

    def rest(c, s):
        stream.fetch(g0 + c + 1)
        s_next = stream.scores(g0 + c + 1)
        stream.update(g0 + c, s)
        return s_next

    s_dec = lax.fori_loop(0, i, body, s0)
    attn_update(i, attn_scores(i, True, range(N_HEADS)), range(N_HEADS))
    s_dec = lax.fori_loop(i, stream.cpr - 1, rest, s_dec)
    stream.update(g0 + stream.cpr - 1, s_dec)
    for h, hs in enumerate(heads):
        o = acc_scr[h] / l_scr[h]
        o_ref[0, :, hs] = (o[:blk] - lam * o[blk:]).astype(o_ref.dtype)
    os_ref[0] = jnp.broadcast_to(stream.finish(lam), os_ref.shape[1:])


def _prompt_attention(page_table, lamp, q, kb, vb, qs, ks_new, vs_new, cache_k, cache_v,
                      lam_init, blk, first_req, ppc):
    b, s, _ = q.shape
    assert s % blk == 0
    nq = s // blk
    n_pages = page_table.shape[1]
    assert n_pages % ppc == 0 and nq <= n_pages // ppc
    stream_cfg = dict(first_req=first_req, n_steps=b * nq, n_pages=n_pages, ppc=ppc)
    kernel = functools.partial(_attn_kernel, blk=blk, nq=nq, lam_init=lam_init, stream_cfg=stream_cfg)
    os_spec, os_shape = _stream_out(nq)
    grid_spec = pltpu.PrefetchScalarGridSpec(
        num_scalar_prefetch=1,
        grid=(b, nq),
        in_specs=[
            pl.BlockSpec((4, HEAD_DIM), lambda bi, i, pt: (0, 0)),
            pl.BlockSpec((1, blk, QK_WIDTH), lambda bi, i, pt: (bi, i, 0)),
            pl.BlockSpec((1, s, QK_WIDTH), lambda bi, i, pt: (bi, 0, 0)),
            pl.BlockSpec((1, s, ATTN_WIDTH), lambda bi, i, pt: (bi, 0, 0)),
        ] + _stream_specs(nq, first_req),
        out_specs=(pl.BlockSpec((1, blk, ATTN_WIDTH), lambda bi, i, pt: (bi, i, 0)), os_spec),
        scratch_shapes=[pltpu.VMEM((N_HEADS, 2 * blk, V_DIM), BF16)]
        + [pltpu.VMEM((N_HEADS, 2 * blk, V_DIM), F32)] * 3 + _stream_scratch(ppc),
    )
    return pl.pallas_call(
        kernel,
        grid_spec=grid_spec,
        out_shape=(jax.ShapeDtypeStruct((b, s, ATTN_WIDTH), BF16), os_shape(b * nq)),
        compiler_params=pltpu.CompilerParams(
            dimension_semantics=("arbitrary", "arbitrary"), vmem_limit_bytes=VMEM_LIMIT_BYTES),
        name="prompt_attn",
    )(page_table.reshape(-1), lamp, q, kb, vb, qs[:, None], ks_new[:, None], vs_new[:, None],
      cache_k, cache_v)


def _mix_residual(h, att, pooled, gsub_ref, wout_ref, sub_scale):
    parts = [_rms(att[:, hd * V_DIM:(hd + 1) * V_DIM], gsub_ref[...]) * sub_scale for hd in range(N_HEADS)]
    mixed = jnp.concatenate(parts + [pooled], axis=1).astype(BF16)
    return h + _dot(mixed, wout_ref[...])


def _pool_project(d_groups, wpool_ref, pscale_ref):
    ys = [_dot(d.astype(BF16), wpool_ref[g]) for g, d in enumerate(d_groups)]
    return jnp.concatenate(ys, axis=1) * pscale_ref[...]


def _gelu_gate(g, val):
    return 0.5 * g * (1.0 + lax.erf(g * (1.0 / math.sqrt(2.0)))) * val


def _ple(h, p, lnp_ref, wpg_ref, wpp_ref):
    gate = jax.nn.sigmoid(_dot(_rms(h, lnp_ref[...]).astype(BF16), wpg_ref[...]))
    return h + gate * _dot(p.astype(BF16), wpp_ref[...])


def _post_prompt_kernel(pt_ref, lamp_ref, h_ref, att_ref, pin_ref, p_ref,
                        qs_ref, ks_ref, vs_ref, ck_hbm, cv_hbm,
                        gsub_ref, wpool_ref, pscale_ref, wout_ref, ln2_ref,
                        wug_ref, wuv_ref, cw_ref, cb_ref, wdn_ref, lnp_ref, wpg_ref, wpp_ref,
                        y_ref, ctail_ref, os_ref,
                        pbuf, ubuf_g, ubuf_v, ucarry, a2_scr, act_scr, *stream_scr,
                        tm, n_tiles, sub_scale, lam_init, stream_cfg):
    t = pl.program_id(1)
    step = pl.program_id(0) * n_tiles + t
    stream = _DecodeStream(pt_ref, ck_hbm, cv_hbm, stream_scr, **stream_cfg)

    @pl.when(step == 0)
    def _():
        stream.prime()

    stream.begin(qs_ref[0].astype(F32), ks_ref[0].astype(F32), vs_ref[0].astype(F32))
    g0 = step * stream.cpr
    stream.fetch(g0)
    s_dec = stream.scores(g0)

    @pl.when(t == 0)
    def _():
        pbuf[0:POOL_HALO, :] = jnp.zeros((POOL_HALO, POOL_WIDTH), F32)
        ucarry[...] = jnp.zeros(ucarry.shape, F32)

    pin = pin_ref[0]
    pbuf[POOL_HALO:, :] = pin
    pos = t * tm + lax.broadcasted_iota(jnp.int32, (tm, 1), 0)
    d_groups = []
    for g, w in enumerate(POOL_WINDOWS):
        gs = slice(g * POOL_GROUP_DIM, (g + 1) * POOL_GROUP_DIM)
        tot = pin[:, gs]
        for back in range(1, w):
            tot = tot + pbuf[POOL_HALO - back:POOL_HALO - back + tm, gs]
        cnt = jnp.minimum(w, pos + 1).astype(F32)
        d_groups.append(tot / cnt - pin[:, gs])
    pbuf[0:POOL_HALO, :] = pbuf[tm:tm + POOL_HALO, :]
    pooled = _pool_project(d_groups, wpool_ref, pscale_ref)

    h1 = _mix_residual(h_ref[0], att_ref[0].astype(F32), pooled, gsub_ref, wout_ref, sub_scale)

    a2_scr[...] = _rms(h1, ln2_ref[...]).astype(BF16)
    halves = ((wug_ref, ubuf_g), (wuv_ref, ubuf_v))

    def columns(half, c):
        return slice(half * FF_PAD + c * FF_CHUNK, half * FF_PAD + (c + 1) * FF_CHUNK)

    def ffn_up(c):
        for half, (w_ref, ubufs) in enumerate(halves):
            ubuf = ubufs.at[c % 2]
            ubuf[0:CONV_HALO, :] = ucarry[:, columns(half, c)]
            ubuf[CONV_HALO:, :] = _dot(a2_scr[...], w_ref[:, c * FF_CHUNK:(c + 1) * FF_CHUNK])

    def ffn_activate(c):
        conv = []
        for half, (_, ubufs) in enumerate(halves):
            ubuf = ubufs.at[c % 2]
            hs = columns(half, c)
            acc = cb_ref[:, hs]
            for j in range(CONV_W):
                lo = CONV_HALO - (CONV_W - 1) + j
                acc = acc + ubuf[lo:lo + tm, :] * cw_ref[j:j + 1, hs]
            ucarry[:, hs] = ubuf[tm:tm + CONV_HALO, :]
            conv.append(acc)
        act_scr[:, c * FF_CHUNK:(c + 1) * FF_CHUNK] = _gelu_gate(conv[0], conv[1]).astype(BF16)

    n_ffn = FF_PAD // FF_CHUNK
    ffn_up(0)
    for c in range(n_ffn):
        nxt = c + 1
        if nxt < stream.cpr:
            stream.fetch(g0 + nxt)
        if nxt < n_ffn:
            ffn_up(nxt)
        if nxt < stream.cpr:
            s_next = stream.scores(g0 + nxt)
        ffn_activate(c)
        if c < stream.cpr:
            stream.update(g0 + c, s_dec)
            s_dec = s_next
    ctail_ref[0] = ucarry[...]
    h2 = h1 + _dot(act_scr[...], wdn_ref[...])

    y_ref[0] = _ple(h2, p_ref[0], lnp_ref, wpg_ref, wpp_ref)
    os_ref[0] = jnp.broadcast_to(stream.finish(_lambda(lamp_ref, lam_init)), os_ref.shape[1:])


def _post_prompt(page_table, lamp, h, att, pin, p, qs, ks_new, vs_new, cache_k, cache_v, weights,
                 sub_scale, lam_init, tm, first_req, ppc):
    b, s, _ = h.shape
    assert s % tm == 0
    n_tiles = s // tm
    n_pages = page_table.shape[1]
    assert n_pages % ppc == 0 and n_pages // ppc <= FF_PAD // FF_CHUNK
    tile = lambda width: pl.BlockSpec((1, tm, width), lambda bi, t, pt: (bi, t, 0))
    resident = lambda arr: pl.BlockSpec(arr.shape, lambda bi, t, pt: (0,) * arr.ndim,
                                        pipeline_mode=pl.Buffered(1))
    stream_cfg = dict(first_req=first_req, n_steps=b * n_tiles, n_pages=n_pages, ppc=ppc)
    kernel = functools.partial(_post_prompt_kernel, tm=tm, n_tiles=n_tiles, sub_scale=sub_scale,
                               lam_init=lam_init, stream_cfg=stream_cfg)
    os_spec, os_shape = _stream_out(n_tiles)
    grid_spec = pltpu.PrefetchScalarGridSpec(
        num_scalar_prefetch=1,
        grid=(b, n_tiles),
        in_specs=[pl.BlockSpec((4, HEAD_DIM), lambda bi, t, pt: (0, 0)),
                  tile(D_MODEL), tile(ATTN_WIDTH), tile(POOL_WIDTH), tile(PLE_DIM)]
        + _stream_specs(n_tiles, first_req) + [resident(w) for w in weights],
        out_specs=(tile(D_MODEL),
                   pl.BlockSpec((1, CONV_HALO, 2 * FF_PAD), lambda bi, t, pt: (bi, 0, 0)),
                   os_spec),
        scratch_shapes=[
            pltpu.VMEM((tm + POOL_HALO, POOL_WIDTH), F32),
            pltpu.VMEM((2, tm + CONV_HALO, FF_CHUNK), F32),
            pltpu.VMEM((2, tm + CONV_HALO, FF_CHUNK), F32),
            pltpu.VMEM((CONV_HALO, 2 * FF_PAD), F32),
            pltpu.VMEM((tm, D_MODEL), BF16),
            pltpu.VMEM((tm, FF_PAD), BF16),
        ] + _stream_scratch(ppc),
    )
    return pl.pallas_call(
        kernel,
        grid_spec=grid_spec,
        out_shape=(jax.ShapeDtypeStruct((b, s, D_MODEL), F32),
                   jax.ShapeDtypeStruct((b, CONV_HALO, 2 * FF_PAD), F32),
                   os_shape(b * n_tiles)),
        compiler_params=pltpu.CompilerParams(
            dimension_semantics=("arbitrary", "arbitrary"), vmem_limit_bytes=VMEM_LIMIT_BYTES),
        name="post_prompt",
    )(page_table.reshape(-1), lamp, h, att, pin, p, qs[:, None], ks_new[:, None], vs_new[:, None],
      cache_k, cache_v, *weights)


def _post_sample_kernel(h_ref, att_ref, pin_ref, p_ref, spool_ref, sconv_ref,
                        gsub_ref, wpool_ref, pscale_ref, wout_ref, ln2_ref,
                        wug_ref, wuv_ref, cw_ref, cb_ref, wdn_ref, lnp_ref, wpg_ref, wpp_ref,
                        y_ref, u_ref, a2_scr, act_scr, *, sub_scale):
    pin = pin_ref[...]
    d_groups = []
    for g, w in enumerate(POOL_WINDOWS):
        gs = slice(g * POOL_GROUP_DIM, (g + 1) * POOL_GROUP_DIM)
        tot = pin[:, gs]
        for back in range(1, w):
            tot = tot + spool_ref[POOL_BUF - back, :, gs]
        d_groups.append(tot / float(w) - pin[:, gs])
    pooled = _pool_project(d_groups, wpool_ref, pscale_ref)

    h1 = _mix_residual(h_ref[...], att_ref[...], pooled, gsub_ref, wout_ref, sub_scale)

    a2_scr[...] = _rms(h1, ln2_ref[...]).astype(BF16)
    for c in range(FF_PAD // FF_CHUNK):
        cs = slice(c * FF_CHUNK, (c + 1) * FF_CHUNK)
        conv = []
        for half, w_ref in enumerate((wug_ref, wuv_ref)):
            hs = slice(half * FF_PAD + c * FF_CHUNK, half * FF_PAD + (c + 1) * FF_CHUNK)
            u = _dot(a2_scr[...], w_ref[:, cs])
            u_ref[:, hs] = u
            acc = cb_ref[:, hs] + u * cw_ref[CONV_W - 1:CONV_W, hs]
            for j in range(CONV_W - 1):
                acc = acc + sconv_ref[j, :, hs] * cw_ref[j:j + 1, hs]
            conv.append(acc)
        act_scr[:, cs] = _gelu_gate(conv[0], conv[1]).astype(BF16)
    h2 = h1 + _dot(act_scr[...], wdn_ref[...])

    y_ref[...] = _ple(h2, p_ref[...], lnp_ref, wpg_ref, wpp_ref)


def _post_sample(h, att, pin, p, spool, sconv, weights, sub_scale):
    n = h.shape[0]
    args = (h, att, pin, p, spool, sconv) + tuple(weights)
    full = lambda arr: pl.BlockSpec(arr.shape, lambda i: (0,) * arr.ndim, pipeline_mode=pl.Buffered(1))
    kernel = functools.partial(_post_sample_kernel, sub_scale=sub_scale)
    return pl.pallas_call(
        kernel,
        grid=(1,),
        in_specs=[full(a) for a in args],
        out_specs=(pl.BlockSpec((n, D_MODEL), lambda i: (0, 0)),
                   pl.BlockSpec((n, 2 * FF_PAD), lambda i: (0, 0))),
        out_shape=(jax.ShapeDtypeStruct((n, D_MODEL), F32),
                   jax.ShapeDtypeStruct((n, 2 * FF_PAD), F32)),
        scratch_shapes=[pltpu.VMEM((n, D_MODEL), BF16), pltpu.VMEM((n, FF_PAD), BF16)],
        compiler_params=pltpu.CompilerParams(
            dimension_semantics=("arbitrary",), vmem_limit_bytes=VMEM_LIMIT_BYTES),
        name="post_sample",
    )(*args)


def _split_ff(x):
    pad = [(0, 0)] * (x.ndim - 1) + [(0, FF_PAD - D_FF)]
    return jnp.concatenate([jnp.pad(x[..., :D_FF], pad), jnp.pad(x[..., D_FF:], pad)], axis=-1)


def _merge_ff(x):
    return jnp.concatenate([x[..., :D_FF], x[..., FF_PAD:FF_PAD + D_FF]], axis=-1)


def _layer(l, h_p, h_s, cache_k, cache_v, state_pool, state_conv, page_table, p_prompt, p_sample,
           ln1, w_in, g_q, g_k, lam_q1, lam_k1, lam_q2, lam_k2, g_sub, w_pool, pool_scale, w_out,
           ln2, w_up, conv_w, conv_b, w_down, ln_ple, w_pg, w_pp):
    b, s, _ = h_p.shape
    n_req = h_s.shape[0]
    n_pool = cache_k.shape[1]
    lam_init = 0.8 - 0.6 * math.exp(-0.3 * l)
    sub_scale = 1.0 - lam_init

    row = lambda x: x[l].reshape(1, -1)
    win_b = w_in[l].astype(BF16)
    gq = jnp.tile(g_q[l], QK_WIDTH // HEAD_DIM).reshape(1, -1) * (HEAD_DIM ** -0.5)
    gk = jnp.tile(g_k[l], QK_WIDTH // HEAD_DIM).reshape(1, -1)
    grp = jnp.arange(MXU_DIM) // HEAD_DIM
    gmat = jnp.where(grp[:, None] == grp[None, :], 1.0 / HEAD_DIM, 0.0).astype(BF16)
    lamp = jnp.stack([lam_q1[l], lam_k1[l], lam_q2[l], lam_k2[l]])
    wup_b = w_up[l].astype(BF16)
    ffpad = ((0, 0), (0, FF_PAD - D_FF))
    tail_weights = (
        row(g_sub), w_pool[l].astype(BF16), row(pool_scale), w_out[l].astype(BF16), row(ln2),
        jnp.pad(wup_b[:, :D_FF], ffpad), jnp.pad(wup_b[:, D_FF:], ffpad),
        _split_ff(conv_w[l]), _split_ff(conv_b[l].reshape(1, -1)),
        jnp.pad(w_down[l].astype(BF16), ((0, FF_PAD - D_FF), (0, 0))),
        row(ln_ple), w_pg[l].astype(BF16), w_pp[l].astype(BF16),
    )

    qs, kts, v4s, pins, kbs, vbs = _project(h_s.reshape(n_req, D_MODEL), row(ln1), win_b, gmat, gq, gk,
                                            tm=n_req, seq=n_req)
    ck_view = jnp.transpose(cache_k[l], (0, 2, 3, 4, 1)).reshape(n_pool, QK_WIDTH, PAGE_SIZE)
    cv_view = cache_v[l].reshape(n_pool, PAGE_ROWS, V_DIM)
    stream_args = (qs, kbs, vbs, ck_view, cv_view)

    q, kt, v4, pin, kb, vb = _project(h_p.reshape(b * s, D_MODEL), row(ln1), win_b, gmat, gq, gk,
                                      tm=PROJ_TILE, seq=s)
    shp = lambda x: x.reshape(b, s, x.shape[-1])
    n_attn_req = b * (s // TOKEN_TILE)
    assert n_req == 2 * n_attn_req, "one decode request per grid step of the two host kernels"
    att, att_s0 = _prompt_attention(page_table, lamp, shp(q), shp(kb), shp(vb), *stream_args,
                                    lam_init, blk=TOKEN_TILE, first_req=0, ppc=STREAM_PAGES)
    y_p, ctail, att_s1 = _post_prompt(page_table, lamp, h_p, att, shp(pin), p_prompt[l], *stream_args,
                                      tail_weights, sub_scale, lam_init, tm=TOKEN_TILE,
                                      first_req=n_attn_req, ppc=STREAM_PAGES)
    untranspose_k = lambda x, n: jnp.transpose(x.reshape(-1, N_HEADS, 2, HEAD_DIM, n), (0, 4, 1, 2, 3))
    k_p = untranspose_k(kt, s)
    v_p = v4.reshape(b, s, N_HEADS, V_DIM)
    pool_p = shp(pin)[:, s - POOL_BUF:]
    conv_p = _merge_ff(ctail[:, CONV_HALO - (CONV_W - 1):])

    att_s = jnp.concatenate([att_s0[:, 0], att_s1[:, 0]], axis=0)
    sconv = jnp.moveaxis(_split_ff(state_conv[l]), 1, 0)
    y_s, u_s = _post_sample(h_s.reshape(n_req, D_MODEL), att_s, pins, p_sample[l].reshape(n_req, PLE_DIM),
                            jnp.moveaxis(state_pool[l], 1, 0), sconv,
                            tail_weights, sub_scale)
    k_s = untranspose_k(kts, n_req).reshape(n_req, 1, N_HEADS, 2, HEAD_DIM)
    v_s = v4s.reshape(n_req, 1, N_HEADS, V_DIM)
    pool_s = jnp.concatenate([state_pool[l][:, 1:], pins[:, None]], axis=1)
    conv_s = jnp.concatenate([state_conv[l][:, 1:], _merge_ff(u_s)[:, None]], axis=1)

    return (y_p, y_s.reshape(n_req, 1, D_MODEL)), (k_p, v_p, pool_p, conv_p, k_s, v_s, pool_s, conv_s)


def kernel(x_prompt, x_sample, cache_k, cache_v, state_pool, state_conv, page_table, p_prompt, p_sample, ln1, w_in, g_q, g_k, lam_q1, lam_k1, lam_q2, lam_k2, g_sub, w_pool, pool_scale, w_out, ln2, w_up, conv_w, conv_b, w_down, ln_ple, w_pg, w_pp):
    depth = ln1.shape[0]
    h_p, h_s = x_prompt, x_sample
    per_layer = []
    for l in range(depth):
        (h_p, h_s), outs = _layer(
            l, h_p, h_s, cache_k, cache_v, state_pool, state_conv, page_table, p_prompt, p_sample,
            ln1, w_in, g_q, g_k, lam_q1, lam_k1, lam_q2, lam_k2, g_sub, w_pool, pool_scale, w_out,
            ln2, w_up, conv_w, conv_b, w_down, ln_ple, w_pg, w_pp)
        per_layer.append(outs)
    stacked = tuple(jnp.stack(xs) for xs in zip(*per_layer))
    return (h_p, h_s) + stacked
```

```python
import functools
import math

import jax
import jax.numpy as jnp
from jax import lax
from jax.experimental import pallas as pl
from jax.experimental.pallas import tpu as pltpu

F32 = jnp.float32
BF16 = jnp.bfloat16

D_MODEL = 1024
N_HEADS = 4
HEAD_DIM = 64
V_DIM = 2 * HEAD_DIM
QK_WIDTH = N_HEADS * 2 * HEAD_DIM
ATTN_WIDTH = N_HEADS * V_DIM
POOL_WINDOWS = (2, 4, 8, 16)
POOL_WIDTH = D_MODEL - ATTN_WIDTH
POOL_GROUP_DIM = POOL_WIDTH // len(POOL_WINDOWS)
POOL_BUF = max(POOL_WINDOWS) - 1
IN_WIDTH = 2 * QK_WIDTH + ATTN_WIDTH + POOL_WIDTH
D_FF = 2752
CONV_W = 3
PLE_DIM = 256
PAGE_SIZE = 128
PAGE_ROWS = PAGE_SIZE * N_HEADS
EPS = 1e-6
NEG = -1e30

LANES = 128
SUBLANES = 8
MXU_DIM = 256
N_DMA_PRIORITIES = 2
STREAM_AHEAD = 3
STREAM_SLOTS = STREAM_AHEAD + 2
VMEM_LIMIT_BYTES = 60 * 1024 * 1024

FF_PAD = -(-D_FF // MXU_DIM) * MXU_DIM
FF_CHUNK = MXU_DIM
TOKEN_TILE = 256
PROJ_TILE = 512
STREAM_PAGES = 8
POOL_HALO = 16
CONV_HALO = SUBLANES


def _rms(x, g):
    ms = jnp.mean(x * x, axis=-1, keepdims=True)
    return x * lax.rsqrt(ms + EPS) * g


def _dot(a, b):
    return jnp.dot(a, b, preferred_element_type=F32)


def _dot_nt(a, b):
    return lax.dot_general(a, b, (((1,), (1,)), ((), ())), preferred_element_type=F32)


def _lambda(lamp_ref, lam_init):
    p = lamp_ref[...]
    s1 = jnp.sum(p[0:1] * p[1:2], axis=-1, keepdims=True)
    s2 = jnp.sum(p[2:3] * p[3:4], axis=-1, keepdims=True)
    return jnp.exp(s1) - jnp.exp(s2) + lam_init


def _proj_kernel(h_ref, ln1_ref, win_ref, gmat_ref, gq_ref, gk_ref,
                 q_ref, k_ref, v_ref, pin_ref, kb_ref, vb_ref, a_scr):
    tm = h_ref.shape[0]
    a_scr[...] = _rms(h_ref[...], ln1_ref[...]).astype(BF16)
    gmat = gmat_ref[...]
    n_chunks = IN_WIDTH // MXU_DIM
    per_seg = QK_WIDTH // MXU_DIM
    for c in range(n_chunks):
        z = _dot(a_scr[...], win_ref[:, c * MXU_DIM:(c + 1) * MXU_DIM])
        seg, off = divmod(c, per_seg)
        sl = slice(off * MXU_DIM, (off + 1) * MXU_DIM)
        if seg < 2:
            ms = _dot((z * z).astype(BF16), gmat)
            y = z * lax.rsqrt(ms + EPS)
            if seg == 0:
                q_ref[:, sl] = (y * gq_ref[:, sl]).astype(BF16)
            else:
                y = y * gk_ref[:, sl]
                k_ref[0, sl, :] = y.T
                kb_ref[:, sl] = y.astype(BF16)
        elif seg == 2:
            for hh in range(MXU_DIM // V_DIM):
                head = off * (MXU_DIM // V_DIM) + hh
                v_ref[pl.ds(head, tm, stride=N_HEADS), :] = z[:, hh * V_DIM:(hh + 1) * V_DIM]
            vb_ref[:, sl] = z.astype(BF16)
        else:
            pin_ref[:, sl] = z


def _project(h2d, ln1, win_b, gmat, gq, gk, tm, seq):
    t = h2d.shape[0]
    assert t % seq == 0 and seq % tm == 0
    tiles = seq // tm
    row = lambda i: (i, 0)
    const = lambda i: (0, 0)
    w512 = pl.BlockSpec((tm, QK_WIDTH), row)
    out_shape = (
        jax.ShapeDtypeStruct((t, QK_WIDTH), BF16),
        jax.ShapeDtypeStruct((t // seq, QK_WIDTH, seq), F32),
        jax.ShapeDtypeStruct((t * N_HEADS, V_DIM), F32),
        jax.ShapeDtypeStruct((t, POOL_WIDTH), F32),
        jax.ShapeDtypeStruct((t, QK_WIDTH), BF16),
        jax.ShapeDtypeStruct((t, ATTN_WIDTH), BF16),
    )
    out_specs = (
        w512,
        pl.BlockSpec((1, QK_WIDTH, tm), lambda i: (i // tiles, 0, i % tiles)),
        pl.BlockSpec((tm * N_HEADS, V_DIM), row),
        w512, w512, w512,
    )
    return pl.pallas_call(
        _proj_kernel,
        grid=(t // tm,),
        in_specs=[
            pl.BlockSpec((tm, D_MODEL), row),
            pl.BlockSpec((1, D_MODEL), const),
            pl.BlockSpec((D_MODEL, IN_WIDTH), const, pipeline_mode=pl.Buffered(1)),
            pl.BlockSpec((MXU_DIM, MXU_DIM), const),
            pl.BlockSpec((1, QK_WIDTH), const),
            pl.BlockSpec((1, QK_WIDTH), const),
        ],
        out_specs=out_specs,
        out_shape=out_shape,
        scratch_shapes=[pltpu.VMEM((tm, D_MODEL), BF16)],
        compiler_params=pltpu.CompilerParams(
            dimension_semantics=("arbitrary",), vmem_limit_bytes=VMEM_LIMIT_BYTES),
        name="proj",
    )(h2d, ln1, win_b, gmat, gq, gk)


def _stream_scratch(ppc):
    return [
        pltpu.VMEM((STREAM_SLOTS, ppc, QK_WIDTH, PAGE_SIZE), F32),
        pltpu.VMEM((STREAM_SLOTS, ppc * PAGE_ROWS, V_DIM), F32),
        pltpu.SemaphoreType.DMA((STREAM_SLOTS, 2)),
        pltpu.VMEM((2 * N_HEADS, QK_WIDTH), BF16),
        pltpu.VMEM((2 * N_HEADS, LANES), F32),
        pltpu.VMEM((2 * N_HEADS, LANES), F32),
        pltpu.VMEM((2 * N_HEADS, ATTN_WIDTH), F32),
    ]


class _DecodeStream:
    def __init__(self, pt_ref, ck_hbm, cv_hbm, scratch, *, first_req, n_steps, n_pages, ppc):
        self.pt_ref, self.ck_hbm, self.cv_hbm = pt_ref, ck_hbm, cv_hbm
        (self.kbuf, self.vbuf, self.sem, self.qbd, self.m, self.l, self.acc) = scratch
        self.first_req, self.n_pages, self.ppc = first_req, n_pages, ppc
        self.cpr = n_pages // ppc
        self.total = n_steps * self.cpr
        self.hc = lax.broadcasted_iota(jnp.int32, (2 * N_HEADS, QK_WIDTH), 0)

    def _copies(self, g, slot, real):
        base = (self.first_req + lax.div(g, self.cpr)) * self.n_pages + lax.rem(g, self.cpr) * self.ppc
        copies = []
        for p in range(self.ppc):
            page = self.pt_ref[base + p] if real else 0
            rows = pl.ds(p * PAGE_ROWS, PAGE_ROWS)
            copies.append(pltpu.make_async_copy(self.ck_hbm.at[page], self.kbuf.at[slot, p],
                                                self.sem.at[slot, 0]))
            copies.append(pltpu.make_async_copy(self.cv_hbm.at[page], self.vbuf.at[slot, rows],
                                                self.sem.at[slot, 1]))
        return copies

    def _start(self, g):
        for n, cp in enumerate(self._copies(g, lax.rem(g, STREAM_SLOTS), True)):
            cp.start(priority=n % N_DMA_PRIORITIES)

    def prime(self):
        for g in range(STREAM_AHEAD):
            self._start(jnp.int32(g))

    def begin(self, q_row, k_row, v_row):
        grp = lax.broadcasted_iota(jnp.int32, self.hc.shape, 1) // HEAD_DIM
        qbd = jnp.where(self.hc == grp, jnp.broadcast_to(q_row, self.hc.shape), 0.0)
        self.qbd[...] = qbd.astype(BF16)
        s_self = jnp.sum(qbd * k_row, axis=-1, keepdims=True)
        self.m[...] = jnp.broadcast_to(s_self, self.m.shape)
        self.l[...] = jnp.ones(self.l.shape, F32)
        self.acc[...] = jnp.broadcast_to(v_row, self.acc.shape)

    def fetch(self, g):
        ahead = g + STREAM_AHEAD

        @pl.when(ahead < self.total)
        def _():
            self._start(ahead)

        for cp in self._copies(g, lax.rem(g, STREAM_SLOTS), False):
            cp.wait()

    def scores(self, g):
        slot = lax.rem(g, STREAM_SLOTS)
        qbd = self.qbd[...]
        return jnp.concatenate(
            [_dot(qbd, self.kbuf[slot, p].astype(BF16)) for p in range(self.ppc)], axis=1)

    def update(self, g, s):
        slot = lax.rem(g, STREAM_SLOTS)
        n_tok = self.ppc * PAGE_SIZE
        m_prev = self.m[...]
        m_new = jnp.maximum(m_prev, jnp.max(s, axis=1, keepdims=True))
        alpha = jnp.exp(m_prev - m_new)
        p = jnp.exp(s - m_new[:, :1])
        self.l[...] = alpha * self.l[...] + jnp.sum(p, axis=1, keepdims=True)
        pb = p.astype(BF16)
        pv = [_dot(pb, self.vbuf[slot, pl.ds(h, n_tok, stride=N_HEADS), :].astype(BF16))
              for h in range(N_HEADS)]
        self.acc[...] = alpha[:, :1] * self.acc[...] + jnp.concatenate(pv, axis=1)
        self.m[...] = m_new

    def finish(self, lam):
        coef = jnp.where(lax.rem(self.hc, 2) == 0, 1.0, -lam)
        head_of_lane = lax.broadcasted_iota(jnp.int32, self.hc.shape, 1) // V_DIM
        keep = head_of_lane == self.hc // 2
        o = jnp.where(keep, coef * self.acc[...] / self.l[:, :1], 0.0)
        return jnp.sum(o, axis=0, keepdims=True)


def _stream_specs(n_steps_per_b, first_req):
    per_req = pl.BlockSpec((1, 1, QK_WIDTH), lambda bi, i, pt: (first_req + bi * n_steps_per_b + i, 0, 0))
    return [per_req, per_req, per_req, pl.BlockSpec(memory_space=pl.ANY), pl.BlockSpec(memory_space=pl.ANY)]


def _stream_out(n_steps_per_b):
    return (pl.BlockSpec((1, SUBLANES, ATTN_WIDTH), lambda bi, i, pt: (bi * n_steps_per_b + i, 0, 0)),
            lambda n: jax.ShapeDtypeStruct((n, SUBLANES, ATTN_WIDTH), F32))


def _mix_residual(h, att, pooled, gsub_ref, wout_ref, sub_scale):
    parts = [_rms(att[:, hd * V_DIM:(hd + 1) * V_DIM], gsub_ref[...]) * sub_scale for hd in range(N_HEADS)]
    mixed = jnp.concatenate(parts + [pooled], axis=1).astype(BF16)
    return h + _dot(mixed, wout_ref[...])


def _pool_project(d_groups, wpool_ref, pscale_ref):
    ys = [_dot(d.astype(BF16), wpool_ref[g]) for g, d in enumerate(d_groups)]
    return jnp.concatenate(ys, axis=1) * pscale_ref[...]


def _gelu_gate(g, val):
    return 0.5 * g * (1.0 + lax.erf(g * (1.0 / math.sqrt(2.0)))) * val


def _ple(h, p, lnp_ref, wpg_ref, wpp_ref):
    gate = jax.nn.sigmoid(_dot(_rms(h, lnp_ref[...]).astype(BF16), wpg_ref[...]))
    return h + gate * _dot(p.astype(BF16), wpp_ref[...])


def _pool_mix(pin, tile_idx, pbuf, h, att, gsub_ref, wpool_ref, pscale_ref, wout_ref, sub_scale):
    tm = pin.shape[0]

    @pl.when(tile_idx == 0)
    def _():
        pbuf[0:POOL_HALO, :] = jnp.zeros((POOL_HALO, POOL_WIDTH), F32)

    pbuf[POOL_HALO:, :] = pin
    pos = tile_idx * tm + lax.broadcasted_iota(jnp.int32, (tm, 1), 0)
    d_groups = []
    for g, w in enumerate(POOL_WINDOWS):
        gs = slice(g * POOL_GROUP_DIM, (g + 1) * POOL_GROUP_DIM)
        tot = pin[:, gs]
        for back in range(1, w):
            tot = tot + pbuf[POOL_HALO - back:POOL_HALO - back + tm, gs]
        cnt = jnp.minimum(w, pos + 1).astype(F32)
        d_groups.append(tot / cnt - pin[:, gs])
    pbuf[0:POOL_HALO, :] = pbuf[tm:tm + POOL_HALO, :]
    pooled = _pool_project(d_groups, wpool_ref, pscale_ref)
    return _mix_residual(h, att, pooled, gsub_ref, wout_ref, sub_scale)


def _attn_kernel(pt_ref, lamp_ref, q_ref, k_ref, v_ref, h_ref, pin_ref, qs_ref, ks_ref, vs_ref, ck_hbm, cv_hbm,
                 gsub_ref, wpool_ref, pscale_ref, wout_ref,
                 h1_ref, os_ref, qq_scr, m_scr, l_scr, acc_scr, pbuf, *stream_scr,
                 blk, nq, lam_init, sub_scale, stream_cfg):
    i = pl.program_id(1)
    t = pl.program_id(0) * nq + i
    lam = _lambda(lamp_ref, lam_init)
    stream = _DecodeStream(pt_ref, ck_hbm, cv_hbm, stream_scr, **stream_cfg)

    @pl.when(t == 0)
    def _():
        stream.prime()

    stream.begin(qs_ref[0].astype(F32), ks_ref[0].astype(F32), vs_ref[0].astype(F32))

    lane = lax.broadcasted_iota(jnp.int32, (blk, V_DIM), 1)
    row = lax.broadcasted_iota(jnp.int32, (2 * blk, blk), 0)
    col = lax.broadcasted_iota(jnp.int32, (2 * blk, blk), 1)
    causal = col <= jnp.where(row >= blk, row - blk, row)
    reps = blk // LANES
    heads = [slice(h * V_DIM, (h + 1) * V_DIM) for h in range(N_HEADS)]

    for h, hs in enumerate(heads):
        qh = q_ref[0, :, hs]
        zero = jnp.zeros_like(qh)
        qq_scr[h, 0:blk, :] = jnp.where(lane < HEAD_DIM, qh, zero)
        qq_scr[h, blk:2 * blk, :] = jnp.where(lane >= HEAD_DIM, qh, zero)
    m_scr[...] = jnp.full(m_scr.shape, -jnp.inf, F32)
    l_scr[...] = jnp.zeros(l_scr.shape, F32)
    acc_scr[...] = jnp.zeros(acc_scr.shape, F32)

    def attn_scores(j, masked, which):
        start = pl.multiple_of(j * blk, blk)
        out = []
        for h in which:
            s = _dot_nt(qq_scr[h], k_ref[0, pl.ds(start, blk), heads[h]])
            out.append(jnp.where(causal, s, NEG) if masked else s)
        return out

    def attn_update(j, scores, which):
        start = pl.multiple_of(j * blk, blk)
        for s, h in zip(scores, which):
            hs = heads[h]
            m_prev = m_scr[h]
            m_new = jnp.maximum(m_prev, jnp.max(s, axis=1, keepdims=True))
            alpha = jnp.exp(m_prev - m_new)
            p = jnp.exp(s - jnp.concatenate([m_new] * reps, axis=1))
            l_scr[h] = alpha * l_scr[h] + jnp.sum(p, axis=1, keepdims=True)
            acc_scr[h] = alpha * acc_scr[h] + _dot(p.astype(BF16), v_ref[0, pl.ds(start, blk), hs])
            m_scr[h] = m_new

    g0 = t * stream.cpr
    stream.fetch(g0)
    s0 = stream.scores(g0)

    def body(j, s):
        stream.fetch(g0 + j + 1)
        half = N_HEADS // 2
        first = attn_scores(j, False, range(half))
        s_next = stream.scores(g0 + j + 1)
        attn_update(j, first, range(half))
        second = attn_scores(j, False, range(half, N_HEADS))
        stream.update(g0 + j, s)
        attn_update(j, second, range(half, N_HEADS))
        return s_next

    def rest(c, s):
        stream.fetch(g0 + c + 1)
        s_next = stream.scores(g0 + c + 1)
        stream.update(g0 + c, s)
        return s_next

    s_dec = lax.fori_loop(0, i, body, s0)
    attn_update(i, attn_scores(i, True, range(N_HEADS)), range(N_HEADS))
    s_dec = lax.fori_loop(i, stream.cpr - 1, rest, s_dec)
    stream.update(g0 + stream.cpr - 1, s_dec)
    att = []
    for h in range(N_HEADS):
        o = acc_scr[h] / l_scr[h]
        att.append(o[:blk] - lam * o[blk:])
    h1_ref[0] = _pool_mix(pin_ref[0], i, pbuf, h_ref[0], jnp.concatenate(att, axis=1),
                          gsub_ref, wpool_ref, pscale_ref, wout_ref, sub_scale)
    os_ref[0] = jnp.broadcast_to(stream.finish(lam), os_ref.shape[1:])


def _prompt_attention(page_table, lamp, q, kb, vb, h, pin, qs, ks_new, vs_new, cache_k, cache_v,
                      mix_weights, lam_init, sub_scale, blk, first_req, ppc):
    b, s, _ = q.shape
    assert s % blk == 0
    nq = s // blk
    n_pages = page_table.shape[1]
    assert n_pages % ppc == 0 and nq <= n_pages // ppc
    stream_cfg = dict(first_req=first_req, n_steps=b * nq, n_pages=n_pages, ppc=ppc)
    kernel = functools.partial(_attn_kernel, blk=blk, nq=nq, lam_init=lam_init, sub_scale=sub_scale,
                               stream_cfg=stream_cfg)
    os_spec, os_shape = _stream_out(nq)
    tile = lambda width: pl.BlockSpec((1, blk, width), lambda bi, i, pt: (bi, i, 0))
    resident = lambda arr: pl.BlockSpec(arr.shape, lambda bi, i, pt: (0,) * arr.ndim,
                                        pipeline_mode=pl.Buffered(1))
    grid_spec = pltpu.PrefetchScalarGridSpec(
        num_scalar_prefetch=1,
        grid=(b, nq),
        in_specs=[
            pl.BlockSpec((4, HEAD_DIM), lambda bi, i, pt: (0, 0)),
            tile(QK_WIDTH),
            pl.BlockSpec((1, s, QK_WIDTH), lambda bi, i, pt: (bi, 0, 0)),
            pl.BlockSpec((1, s, ATTN_WIDTH), lambda bi, i, pt: (bi, 0, 0)),
            tile(D_MODEL), tile(POOL_WIDTH),
        ] + _stream_specs(nq, first_req) + [resident(w) for w in mix_weights],
        out_specs=(tile(D_MODEL), os_spec),
        scratch_shapes=[pltpu.VMEM((N_HEADS, 2 * blk, V_DIM), BF16)]
        + [pltpu.VMEM((N_HEADS, 2 * blk, V_DIM), F32)] * 3
        + [pltpu.VMEM((blk + POOL_HALO, POOL_WIDTH), F32)] + _stream_scratch(ppc),
    )
    return pl.pallas_call(
        kernel,
        grid_spec=grid_spec,
        out_shape=(jax.ShapeDtypeStruct((b, s, D_MODEL), F32), os_shape(b * nq)),
        compiler_params=pltpu.CompilerParams(
            dimension_semantics=("arbitrary", "arbitrary"), vmem_limit_bytes=VMEM_LIMIT_BYTES),
        name="prompt_attn",
    )(page_table.reshape(-1), lamp, q, kb, vb, h, pin, qs[:, None], ks_new[:, None], vs_new[:, None],
      cache_k, cache_v, *mix_weights)


def _post_prompt_kernel(pt_ref, lamp_ref, h1_ref, p_ref,
                        qs_ref, ks_ref, vs_ref, ck_hbm, cv_hbm,
                        ln2_ref, wug_ref, wuv_ref, cw_ref, cb_ref, wdn_ref, lnp_ref, wpg_ref, wpp_ref,
                        y_ref, ctail_ref, os_ref,
                        ubuf_g, ubuf_v, ucarry, a2_scr, act_scr, *stream_scr,
                        tm, n_tiles, lam_init, stream_cfg):
    t = pl.program_id(1)
    step = pl.program_id(0) * n_tiles + t
    stream = _DecodeStream(pt_ref, ck_hbm, cv_hbm, stream_scr, **stream_cfg)

    @pl.when(step == 0)
    def _():
        stream.prime()

    stream.begin(qs_ref[0].astype(F32), ks_ref[0].astype(F32), vs_ref[0].astype(F32))
    g0 = step * stream.cpr
    stream.fetch(g0)
    s_dec = stream.scores(g0)

    @pl.when(t == 0)
    def _():
        ucarry[...] = jnp.zeros(ucarry.shape, F32)

    h1 = h1_ref[0]

    a2_scr[...] = _rms(h1, ln2_ref[...]).astype(BF16)
    halves = ((wug_ref, ubuf_g), (wuv_ref, ubuf_v))

    def columns(half, c):
        return slice(half * FF_PAD + c * FF_CHUNK, half * FF_PAD + (c + 1) * FF_CHUNK)

    def ffn_up(c):
        for half, (w_ref, ubufs) in enumerate(halves):
            ubuf = ubufs.at[c % 2]
            ubuf[0:CONV_HALO, :] = ucarry[:, columns(half, c)]
            ubuf[CONV_HALO:, :] = _dot(a2_scr[...], w_ref[:, c * FF_CHUNK:(c + 1) * FF_CHUNK])

    def ffn_activate(c):
        conv = []
        for half, (_, ubufs) in enumerate(halves):
            ubuf = ubufs.at[c % 2]
            hs = columns(half, c)
            acc = cb_ref[:, hs]
            for j in range(CONV_W):
                lo = CONV_HALO - (CONV_W - 1) + j
                acc = acc + ubuf[lo:lo + tm, :] * cw_ref[j:j + 1, hs]
            ucarry[:, hs] = ubuf[tm:tm + CONV_HALO, :]
            conv.append(acc)
        act_scr[:, c * FF_CHUNK:(c + 1) * FF_CHUNK] = _gelu_gate(conv[0], conv[1]).astype(BF16)

    n_ffn = FF_PAD // FF_CHUNK
    ffn_up(0)
    for c in range(n_ffn):
        nxt = c + 1
        if nxt < stream.cpr:
            stream.fetch(g0 + nxt)
        if nxt < n_ffn:
            ffn_up(nxt)
        if nxt < stream.cpr:
            s_next = stream.scores(g0 + nxt)
        ffn_activate(c)
        if c < stream.cpr:
            stream.update(g0 + c, s_dec)
            s_dec = s_next
    ctail_ref[0] = ucarry[...]
    h2 = h1 + _dot(act_scr[...], wdn_ref[...])

    y_ref[0] = _ple(h2, p_ref[0], lnp_ref, wpg_ref, wpp_ref)
    os_ref[0] = jnp.broadcast_to(stream.finish(_lambda(lamp_ref, lam_init)), os_ref.shape[1:])


def _post_prompt(page_table, lamp, h1, p, qs, ks_new, vs_new, cache_k, cache_v, ffn_weights,
                 lam_init, tm, first_req, ppc):
    b, s, _ = h1.shape
    assert s % tm == 0
    n_tiles = s // tm
    n_pages = page_table.shape[1]
    assert n_pages % ppc == 0 and n_pages // ppc <= FF_PAD // FF_CHUNK
    tile = lambda width: pl.BlockSpec((1, tm, width), lambda bi, t, pt: (bi, t, 0))
    resident = lambda arr: pl.BlockSpec(arr.shape, lambda bi, t, pt: (0,) * arr.ndim,
                                        pipeline_mode=pl.Buffered(1))
    stream_cfg = dict(first_req=first_req, n_steps=b * n_tiles, n_pages=n_pages, ppc=ppc)
    kernel = functools.partial(_post_prompt_kernel, tm=tm, n_tiles=n_tiles, lam_init=lam_init,
                               stream_cfg=stream_cfg)
    os_spec, os_shape = _stream_out(n_tiles)
    grid_spec = pltpu.PrefetchScalarGridSpec(
        num_scalar_prefetch=1,
        grid=(b, n_tiles),
        in_specs=[pl.BlockSpec((4, HEAD_DIM), lambda bi, t, pt: (0, 0)), tile(D_MODEL), tile(PLE_DIM)]
        + _stream_specs(n_tiles, first_req) + [resident(w) for w in ffn_weights],
        out_specs=(tile(D_MODEL),
                   pl.BlockSpec((1, CONV_HALO, 2 * FF_PAD), lambda bi, t, pt: (bi, 0, 0)),
                   os_spec),
        scratch_shapes=[
            pltpu.VMEM((2, tm + CONV_HALO, FF_CHUNK), F32),
            pltpu.VMEM((2, tm + CONV_HALO, FF_CHUNK), F32),
            pltpu.VMEM((CONV_HALO, 2 * FF_PAD), F32),
            pltpu.VMEM((tm, D_MODEL), BF16),
            pltpu.VMEM((tm, FF_PAD), BF16),
        ] + _stream_scratch(ppc),
    )
    return pl.pallas_call(
        kernel,
        grid_spec=grid_spec,
        out_shape=(jax.ShapeDtypeStruct((b, s, D_MODEL), F32),
                   jax.ShapeDtypeStruct((b, CONV_HALO, 2 * FF_PAD), F32),
                   os_shape(b * n_tiles)),
        compiler_params=pltpu.CompilerParams(
            dimension_semantics=("arbitrary", "arbitrary"), vmem_limit_bytes=VMEM_LIMIT_BYTES),
        name="post_prompt",
    )(page_table.reshape(-1), lamp, h1, p, qs[:, None], ks_new[:, None], vs_new[:, None],
      cache_k, cache_v, *ffn_weights)


def _post_sample_kernel(h_ref, att_ref, pin_ref, p_ref, spool_ref, sconv_ref,
                        gsub_ref, wpool_ref, pscale_ref, wout_ref, ln2_ref,
                        wug_ref, wuv_ref, cw_ref, cb_ref, wdn_ref, lnp_ref, wpg_ref, wpp_ref,
                        y_ref, u_ref, a2_scr, act_scr, *, sub_scale):
    pin = pin_ref[...]
    d_groups = []
    for g, w in enumerate(POOL_WINDOWS):
        gs = slice(g * POOL_GROUP_DIM, (g + 1) * POOL_GROUP_DIM)
        tot = pin[:, gs]
        for back in range(1, w):
            tot = tot + spool_ref[POOL_BUF - back, :, gs]
        d_groups.append(tot / float(w) - pin[:, gs])
    pooled = _pool_project(d_groups, wpool_ref, pscale_ref)

    h1 = _mix_residual(h_ref[...], att_ref[...], pooled, gsub_ref, wout_ref, sub_scale)

    a2_scr[...] = _rms(h1, ln2_ref[...]).astype(BF16)
    for c in range(FF_PAD // FF_CHUNK):
        cs = slice(c * FF_CHUNK, (c + 1) * FF_CHUNK)
        conv = []
        for half, w_ref in enumerate((wug_ref, wuv_ref)):
            hs = slice(half * FF_PAD + c * FF_CHUNK, half * FF_PAD + (c + 1) * FF_CHUNK)
            u = _dot(a2_scr[...], w_ref[:, cs])
            u_ref[:, hs] = u
            acc = cb_ref[:, hs] + u * cw_ref[CONV_W - 1:CONV_W, hs]
            for j in range(CONV_W - 1):
                acc = acc + sconv_ref[j, :, hs] * cw_ref[j:j + 1, hs]
            conv.append(acc)
        act_scr[:, cs] = _gelu_gate(conv[0], conv[1]).astype(BF16)
    h2 = h1 + _dot(act_scr[...], wdn_ref[...])

    y_ref[...] = _ple(h2, p_ref[...], lnp_ref, wpg_ref, wpp_ref)


def _post_sample(h, att, pin, p, spool, sconv, weights, sub_scale):
    n = h.shape[0]
    args = (h, att, pin, p, spool, sconv) + tuple(weights)
    full = lambda arr: pl.BlockSpec(arr.shape, lambda i: (0,) * arr.ndim, pipeline_mode=pl.Buffered(1))
    kernel = functools.partial(_post_sample_kernel, sub_scale=sub_scale)
    return pl.pallas_call(
        kernel,
        grid=(1,),
        in_specs=[full(a) for a in args],
        out_specs=(pl.BlockSpec((n, D_MODEL), lambda i: (0, 0)),
                   pl.BlockSpec((n, 2 * FF_PAD), lambda i: (0, 0))),
        out_shape=(jax.ShapeDtypeStruct((n, D_MODEL), F32),
                   jax.ShapeDtypeStruct((n, 2 * FF_PAD), F32)),
        scratch_shapes=[pltpu.VMEM((n, D_MODEL), BF16), pltpu.VMEM((n, FF_PAD), BF16)],
        compiler_params=pltpu.CompilerParams(
            dimension_semantics=("arbitrary",), vmem_limit_bytes=VMEM_LIMIT_BYTES),
        name="post_sample",
    )(*args)


def _split_ff(x):
    pad = [(0, 0)] * (x.ndim - 1) + [(0, FF_PAD - D_FF)]
    return jnp.concatenate([jnp.pad(x[..., :D_FF], pad), jnp.pad(x[..., D_FF:], pad)], axis=-1)


def _merge_ff(x):
    return jnp.concatenate([x[..., :D_FF], x[..., FF_PAD:FF_PAD + D_FF]], axis=-1)


def _layer(l, h_p, h_s, cache_k, cache_v, state_pool, state_conv, page_table, p_prompt, p_sample,
           ln1, w_in, g_q, g_k, lam_q1, lam_k1, lam_q2, lam_k2, g_sub, w_pool, pool_scale, w_out,
           ln2, w_up, conv_w, conv_b, w_down, ln_ple, w_pg, w_pp):
    b, s, _ = h_p.shape
    n_req = h_s.shape[0]
    n_pool = cache_k.shape[1]
    lam_init = 0.8 - 0.6 * math.exp(-0.3 * l)
    sub_scale = 1.0 - lam_init

    row = lambda x: x[l].reshape(1, -1)
    win_b = w_in[l].astype(BF16)
    gq = jnp.tile(g_q[l], QK_WIDTH // HEAD_DIM).reshape(1, -1) * (HEAD_DIM ** -0.5)
    gk = jnp.tile(g_k[l], QK_WIDTH // HEAD_DIM).reshape(1, -1)
    grp = jnp.arange(MXU_DIM) // HEAD_DIM
    gmat = jnp.where(grp[:, None] == grp[None, :], 1.0 / HEAD_DIM, 0.0).astype(BF16)
    lamp = jnp.stack([lam_q1[l], lam_k1[l], lam_q2[l], lam_k2[l]])
    wup_b = w_up[l].astype(BF16)
    ffpad = ((0, 0), (0, FF_PAD - D_FF))
    tail_weights = (
        row(g_sub), w_pool[l].astype(BF16), row(pool_scale), w_out[l].astype(BF16), row(ln2),
        jnp.pad(wup_b[:, :D_FF], ffpad), jnp.pad(wup_b[:, D_FF:], ffpad),
        _split_ff(conv_w[l]), _split_ff(conv_b[l].reshape(1, -1)),
        jnp.pad(w_down[l].astype(BF16), ((0, FF_PAD - D_FF), (0, 0))),
        row(ln_ple), w_pg[l].astype(BF16), w_pp[l].astype(BF16),
    )

    qs, kts, v4s, pins, kbs, vbs = _project(h_s.reshape(n_req, D_MODEL), row(ln1), win_b, gmat, gq, gk,
                                            tm=n_req, seq=n_req)
    ck_view = jnp.transpose(cache_k[l], (0, 2, 3, 4, 1)).reshape(n_pool, QK_WIDTH, PAGE_SIZE)
    cv_view = cache_v[l].reshape(n_pool, PAGE_ROWS, V_DIM)
    stream_args = (qs, kbs, vbs, ck_view, cv_view)

    q, kt, v4, pin, kb, vb = _project(h_p.reshape(b * s, D_MODEL), row(ln1), win_b, gmat, gq, gk,
                                      tm=PROJ_TILE, seq=s)
    shp = lambda x: x.reshape(b, s, x.shape[-1])
    n_attn_req = b * (s // TOKEN_TILE)
    assert n_req == 2 * n_attn_req, "one decode request per grid step of the two host kernels"
    n_mix = 4
    h1, att_s0 = _prompt_attention(page_table, lamp, shp(q), shp(kb), shp(vb), h_p, shp(pin), *stream_args,
                                   tail_weights[:n_mix], lam_init, sub_scale, blk=TOKEN_TILE,
                                   first_req=0, ppc=STREAM_PAGES)
    y_p, ctail, att_s1 = _post_prompt(page_table, lamp, h1, p_prompt[l], *stream_args,
                                      tail_weights[n_mix:], lam_init, tm=TOKEN_TILE,
                                      first_req=n_attn_req, ppc=STREAM_PAGES)
    untranspose_k = lambda x, n: jnp.transpose(x.reshape(-1, N_HEADS, 2, HEAD_DIM, n), (0, 4, 1, 2, 3))
    k_p = untranspose_k(kt, s)
    v_p = v4.reshape(b, s, N_HEADS, V_DIM)
    pool_p = shp(pin)[:, s - POOL_BUF:]
    conv_p = _merge_ff(ctail[:, CONV_HALO - (CONV_W - 1):])

    att_s = jnp.concatenate([att_s0[:, 0], att_s1[:, 0]], axis=0)
    sconv = jnp.moveaxis(_split_ff(state_conv[l]), 1, 0)
    y_s, u_s = _post_sample(h_s.reshape(n_req, D_MODEL), att_s, pins, p_sample[l].reshape(n_req, PLE_DIM),
                            jnp.moveaxis(state_pool[l], 1, 0), sconv,
                            tail_weights, sub_scale)
    k_s = untranspose_k(kts, n_req).reshape(n_req, 1, N_HEADS, 2, HEAD_DIM)
    v_s = v4s.reshape(n_req, 1, N_HEADS, V_DIM)
    pool_s = jnp.concatenate([state_pool[l][:, 1:], pins[:, None]], axis=1)
    conv_s = jnp.concatenate([state_conv[l][:, 1:], _merge_ff(u_s)[:, None]], axis=1)

    return (y_p, y_s.reshape(n_req, 1, D_MODEL)), (k_p, v_p, pool_p, conv_p, k_s, v_s, pool_s, conv_s)


def kernel(x_prompt, x_sample, cache_k, cache_v, state_pool, state_conv, page_table, p_prompt, p_sample, ln1, w_in, g_q, g_k, lam_q1, lam_k1, lam_q2, lam_k2, g_sub, w_pool, pool_scale, w_out, ln2, w_up, conv_w, conv_b, w_down, ln_ple, w_pg, w_pp):
    depth = ln1.shape[0]
    h_p, h_s = x_prompt, x_sample
    per_layer = []
    for l in range(depth):
        (h_p, h_s), outs = _layer(
            l, h_p, h_s, cache_k, cache_v, state_pool, state_conv, page_table, p_prompt, p_sample,
            ln1, w_in, g_q, g_k, lam_q1, lam_k1, lam_q2, lam_k2, g_sub, w_pool, pool_scale, w_out,
            ln2, w_up, conv_w, conv_b, w_down, ln_ple, w_pg, w_pp)
        per_layer.append(outs)
    stacked = tuple(jnp.stack(xs) for xs in zip(*per_layer))
    return (h_p, h_s) + stacked
```

```python
import functools
import math

import jax
import jax.numpy as jnp
from jax import lax
from jax.experimental import pallas as pl
from jax.experimental.pallas import tpu as pltpu

F32 = jnp.float32
BF16 = jnp.bfloat16

D_MODEL = 1024
N_HEADS = 4
HEAD_DIM = 64
V_DIM = 2 * HEAD_DIM
QK_WIDTH = N_HEADS * 2 * HEAD_DIM
ATTN_WIDTH = N_HEADS * V_DIM
POOL_WINDOWS = (2, 4, 8, 16)
POOL_WIDTH = D_MODEL - ATTN_WIDTH
POOL_GROUP_DIM = POOL_WIDTH // len(POOL_WINDOWS)
POOL_BUF = max(POOL_WINDOWS) - 1
IN_WIDTH = 2 * QK_WIDTH + ATTN_WIDTH + POOL_WIDTH
D_FF = 2752
CONV_W = 3
PLE_DIM = 256
PAGE_SIZE = 128
PAGE_ROWS = PAGE_SIZE * N_HEADS
EPS = 1e-6
NEG = -1e30

LANES = 128
SUBLANES = 8
MXU_DIM = 256
N_DMA_PRIORITIES = 2
STREAM_AHEAD = 3
STREAM_SLOTS = STREAM_AHEAD + 2
VMEM_LIMIT_BYTES = 60 * 1024 * 1024

FF_PAD = -(-D_FF // MXU_DIM) * MXU_DIM
FF_CHUNK = MXU_DIM
TOKEN_TILE = 256
PROJ_TILE = 512
POST_STEPS_PER_REQ = 2
STREAM_PAGES = 8
POOL_HALO = 16
CONV_HALO = SUBLANES


def _rms(x, g):
    ms = jnp.mean(x * x, axis=-1, keepdims=True)
    return x * lax.rsqrt(ms + EPS) * g


def _dot(a, b):
    return jnp.dot(a, b, preferred_element_type=F32)


def _dot_nt(a, b):
    return lax.dot_general(a, b, (((1,), (1,)), ((), ())), preferred_element_type=F32)


def _lambda(lamp_ref, lam_init):
    p = lamp_ref[...]
    s1 = jnp.sum(p[0:1] * p[1:2], axis=-1, keepdims=True)
    s2 = jnp.sum(p[2:3] * p[3:4], axis=-1, keepdims=True)
    return jnp.exp(s1) - jnp.exp(s2) + lam_init


def _proj_kernel(*refs, stream_cfg, lam_init):
    if stream_cfg is None:
        (h_ref, ln1_ref, win_ref, gmat_ref, gq_ref, gk_ref,
         q_ref, k_ref, v_ref, pin_ref, kb_ref, vb_ref, a_scr) = refs
        stream = None
    else:
        (pt_ref, h_ref, ln1_ref, win_ref, gmat_ref, gq_ref, gk_ref, lamp_ref,
         qs_ref, ks_ref, vs_ref, ck_hbm, cv_hbm,
         q_ref, k_ref, v_ref, pin_ref, kb_ref, vb_ref, os_ref, a_scr, *stream_scr) = refs
        step = pl.program_id(0)
        stream = _DecodeStream(pt_ref, ck_hbm, cv_hbm, stream_scr, **stream_cfg)

        @pl.when(step == 0)
        def _():
            stream.prime()

        g0 = stream.begin_step(step, qs_ref, ks_ref, vs_ref)
        stream.fetch(g0)
        s_dec = stream.scores(g0)

    tm = h_ref.shape[0]
    a_scr[...] = _rms(h_ref[...], ln1_ref[...]).astype(BF16)
    gmat = gmat_ref[...]
    n_chunks = IN_WIDTH // MXU_DIM
    per_seg = QK_WIDTH // MXU_DIM
    for c in range(n_chunks):
        hosted = stream is not None and c < stream.chunks_per_step
        prefetch = stream is not None and c + 1 < stream.chunks_per_step
        if prefetch:
            stream.fetch(g0 + c + 1)
        z = _dot(a_scr[...], win_ref[:, c * MXU_DIM:(c + 1) * MXU_DIM])
        seg, off = divmod(c, per_seg)
        sl = slice(off * MXU_DIM, (off + 1) * MXU_DIM)
        if seg < 2:
            ms = _dot((z * z).astype(BF16), gmat)
            y = z * lax.rsqrt(ms + EPS)
            if seg == 0:
                q_ref[:, sl] = (y * gq_ref[:, sl]).astype(BF16)
            else:
                y = y * gk_ref[:, sl]
                k_ref[0, sl, :] = y.T
                kb_ref[:, sl] = y.astype(BF16)
        elif seg == 2:
            for hh in range(MXU_DIM // V_DIM):
                head = off * (MXU_DIM // V_DIM) + hh
                v_ref[pl.ds(head, tm, stride=N_HEADS), :] = z[:, hh * V_DIM:(hh + 1) * V_DIM]
            vb_ref[:, sl] = z.astype(BF16)
        else:
            pin_ref[:, sl] = z
        if prefetch:
            s_next = stream.scores(g0 + c + 1)
        if hosted:
            stream.update(g0 + c, s_dec)
            s_dec = s_next if prefetch else None
    if stream is not None:
        stream.finish_step(step, _lambda(lamp_ref, lam_init), os_ref)


def _project(h2d, ln1, win_b, gmat, gq, gk, tm, seq, host=None):
    t = h2d.shape[0]
    assert t % seq == 0 and seq % tm == 0
    tiles = seq // tm
    n_steps = t // tm
    row = lambda i, *_: (i, 0)
    const = lambda i, *_: (0, 0)
    w512 = pl.BlockSpec((tm, QK_WIDTH), row)
    out_shape = [
        jax.ShapeDtypeStruct((t, QK_WIDTH), BF16),
        jax.ShapeDtypeStruct((t // seq, QK_WIDTH, seq), F32),
        jax.ShapeDtypeStruct((t * N_HEADS, V_DIM), F32),
        jax.ShapeDtypeStruct((t, POOL_WIDTH), F32),
        jax.ShapeDtypeStruct((t, QK_WIDTH), BF16),
        jax.ShapeDtypeStruct((t, ATTN_WIDTH), BF16),
    ]
    out_specs = [
        w512,
        pl.BlockSpec((1, QK_WIDTH, tm), lambda i, *_: (i // tiles, 0, i % tiles)),
        pl.BlockSpec((tm * N_HEADS, V_DIM), row),
        w512, w512, w512,
    ]
    in_specs = [
        pl.BlockSpec((tm, D_MODEL), row),
        pl.BlockSpec((1, D_MODEL), const),
        pl.BlockSpec((D_MODEL, IN_WIDTH), const, pipeline_mode=pl.Buffered(1)),
        pl.BlockSpec((MXU_DIM, MXU_DIM), const),
        pl.BlockSpec((1, QK_WIDTH), const),
        pl.BlockSpec((1, QK_WIDTH), const),
    ]
    scratch = [pltpu.VMEM((tm, D_MODEL), BF16)]
    args = (h2d, ln1, win_b, gmat, gq, gk)
    params = pltpu.CompilerParams(dimension_semantics=("arbitrary",), vmem_limit_bytes=VMEM_LIMIT_BYTES)
    if host is None:
        kernel = functools.partial(_proj_kernel, stream_cfg=None, lam_init=None)
        return pl.pallas_call(
            kernel, grid=(n_steps,), in_specs=in_specs, out_specs=out_specs, out_shape=out_shape,
            scratch_shapes=scratch, compiler_params=params, name="proj",
        )(*args)

    page_table, lamp, qs, ks_new, vs_new, cache_k, cache_v, lam_init, first_req, ppc = host
    n_pages = page_table.shape[1]
    assert n_pages % ppc == 0 and n_pages // ppc <= IN_WIDTH // MXU_DIM
    stream_cfg = dict(first_req=first_req, n_steps=n_steps, n_pages=n_pages, ppc=ppc)
    step_of = lambda i: i
    os_spec, os_shape = _stream_out(step_of, n_steps)
    grid_spec = pltpu.PrefetchScalarGridSpec(
        num_scalar_prefetch=1,
        grid=(n_steps,),
        in_specs=in_specs + [pl.BlockSpec((4, HEAD_DIM), const)] + _stream_specs(step_of, first_req),
        out_specs=out_specs + [os_spec],
        scratch_shapes=scratch + _stream_scratch(ppc),
    )
    kernel = functools.partial(_proj_kernel, stream_cfg=stream_cfg, lam_init=lam_init)
    return pl.pallas_call(
        kernel, grid_spec=grid_spec, out_shape=out_shape + [os_shape], compiler_params=params,
        name="proj_host",
    )(page_table.reshape(-1), *args, lamp, qs[:, None], ks_new[:, None], vs_new[:, None], cache_k, cache_v)


def _stream_scratch(ppc):
    return [
        pltpu.VMEM((STREAM_SLOTS, ppc, QK_WIDTH, PAGE_SIZE), F32),
        pltpu.VMEM((STREAM_SLOTS, ppc * PAGE_ROWS, V_DIM), F32),
        pltpu.SemaphoreType.DMA((STREAM_SLOTS, 2)),
        pltpu.VMEM((2 * N_HEADS, QK_WIDTH), BF16),
        pltpu.VMEM((2 * N_HEADS, LANES), F32),
        pltpu.VMEM((2 * N_HEADS, LANES), F32),
        pltpu.VMEM((2 * N_HEADS, ATTN_WIDTH), F32),
    ]


class _DecodeStream:
    def __init__(self, pt_ref, ck_hbm, cv_hbm, scratch, *, first_req, n_steps, n_pages, ppc, steps_per_req=1):
        self.pt_ref, self.ck_hbm, self.cv_hbm = pt_ref, ck_hbm, cv_hbm
        (self.kbuf, self.vbuf, self.sem, self.qbd, self.m, self.l, self.acc) = scratch
        self.first_req, self.n_pages, self.ppc = first_req, n_pages, ppc
        self.cpr = n_pages // ppc
        assert self.cpr % steps_per_req == 0 and n_steps % steps_per_req == 0
        self.steps_per_req = steps_per_req
        self.chunks_per_step = self.cpr // steps_per_req
        self.total = n_steps * self.chunks_per_step
        self.hc = lax.broadcasted_iota(jnp.int32, (2 * N_HEADS, QK_WIDTH), 0)

    def _copies(self, g, slot, real):
        base = (self.first_req + lax.div(g, self.cpr)) * self.n_pages + lax.rem(g, self.cpr) * self.ppc
        copies = []
        for p in range(self.ppc):
            page = self.pt_ref[base + p] if real else 0
            rows = pl.ds(p * PAGE_ROWS, PAGE_ROWS)
            copies.append(pltpu.make_async_copy(self.ck_hbm.at[page], self.kbuf.at[slot, p],
                                                self.sem.at[slot, 0]))
            copies.append(pltpu.make_async_copy(self.cv_hbm.at[page], self.vbuf.at[slot, rows],
                                                self.sem.at[slot, 1]))
        return copies

    def _start(self, g):
        for n, cp in enumerate(self._copies(g, lax.rem(g, STREAM_SLOTS), True)):
            cp.start(priority=n % N_DMA_PRIORITIES)

    def prime(self):
        for g in range(STREAM_AHEAD):
            self._start(jnp.int32(g))

    def begin(self, q_row, k_row, v_row):
        grp = lax.broadcasted_iota(jnp.int32, self.hc.shape, 1) // HEAD_DIM
        qbd = jnp.where(self.hc == grp, jnp.broadcast_to(q_row, self.hc.shape), 0.0)
        self.qbd[...] = qbd.astype(BF16)
        s_self = jnp.sum(qbd * k_row, axis=-1, keepdims=True)
        self.m[...] = jnp.broadcast_to(s_self, self.m.shape)
        self.l[...] = jnp.ones(self.l.shape, F32)
        self.acc[...] = jnp.broadcast_to(v_row, self.acc.shape)

    def begin_step(self, step, q_ref, k_ref, v_ref):
        if self.steps_per_req == 1:
            self.begin(q_ref[0].astype(F32), k_ref[0].astype(F32), v_ref[0].astype(F32))
        else:
            @pl.when(lax.rem(step, self.steps_per_req) == 0)
            def _():
                self.begin(q_ref[0].astype(F32), k_ref[0].astype(F32), v_ref[0].astype(F32))
        return step * self.chunks_per_step

    def finish_step(self, step, lam, o_ref):
        def write():
            o_ref[0] = jnp.broadcast_to(self.finish(lam), o_ref.shape[1:])

        if self.steps_per_req == 1:
            write()
        else:
            pl.when(lax.rem(step, self.steps_per_req) == self.steps_per_req - 1)(write)

    def fetch(self, g):
        ahead = g + STREAM_AHEAD

        @pl.when(ahead < self.total)
        def _():
            self._start(ahead)

        for cp in self._copies(g, lax.rem(g, STREAM_SLOTS), False):
            cp.wait()

    def scores(self, g):
        slot = lax.rem(g, STREAM_SLOTS)
        qbd = self.qbd[...]
        return jnp.concatenate(
            [_dot(qbd, self.kbuf[slot, p].astype(BF16)) for p in range(self.ppc)], axis=1)

    def update(self, g, s):
        slot = lax.rem(g, STREAM_SLOTS)
        n_tok = self.ppc * PAGE_SIZE
        m_prev = self.m[...]
        m_new = jnp.maximum(m_prev, jnp.max(s, axis=1, keepdims=True))
        alpha = jnp.exp(m_prev - m_new)
        p = jnp.exp(s - m_new[:, :1])
        self.l[...] = alpha * self.l[...] + jnp.sum(p, axis=1, keepdims=True)
        pb = p.astype(BF16)
        pv = [_dot(pb, self.vbuf[slot, pl.ds(h, n_tok, stride=N_HEADS), :].astype(BF16))
              for h in range(N_HEADS)]
        self.acc[...] = alpha[:, :1] * self.acc[...] + jnp.concatenate(pv, axis=1)
        self.m[...] = m_new

    def finish(self, lam):
        coef = jnp.where(lax.rem(self.hc, 2) == 0, 1.0, -lam)
        head_of_lane = lax.broadcasted_iota(jnp.int32, self.hc.shape, 1) // V_DIM
        keep = head_of_lane == self.hc // 2
        o = jnp.where(keep, coef * self.acc[...] / self.l[:, :1], 0.0)
        return jnp.sum(o, axis=0, keepdims=True)


def _stream_specs(step_of, first_req, steps_per_req=1):
    per_req = pl.BlockSpec((1, 1, QK_WIDTH),
                           lambda *ids: (first_req + step_of(*ids[:-1]) // steps_per_req, 0, 0))
    return [per_req, per_req, per_req, pl.BlockSpec(memory_space=pl.ANY), pl.BlockSpec(memory_space=pl.ANY)]


def _stream_out(step_of, n_steps, steps_per_req=1):
    return (pl.BlockSpec((1, SUBLANES, ATTN_WIDTH), lambda *ids: (step_of(*ids[:-1]) // steps_per_req, 0, 0)),
            jax.ShapeDtypeStruct((n_steps // steps_per_req, SUBLANES, ATTN_WIDTH), F32))


def _mix_residual(h, att, pooled, gsub_ref, wout_ref, sub_scale):
    parts = [_rms(att[:, hd * V_DIM:(hd + 1) * V_DIM], gsub_ref[...]) * sub_scale for hd in range(N_HEADS)]
    mixed = jnp.concatenate(parts + [pooled], axis=1).astype(BF16)
    return h + _dot(mixed, wout_ref[...])


def _pool_project(d_groups, wpool_ref, pscale_ref):
    ys = [_dot(d.astype(BF16), wpool_ref[g]) for g, d in enumerate(d_groups)]
    return jnp.concatenate(ys, axis=1) * pscale_ref[...]


def _gelu_gate(g, val):
    return 0.5 * g * (1.0 + lax.erf(g * (1.0 / math.sqrt(2.0)))) * val


def _ple(h, p, lnp_ref, wpg_ref, wpp_ref):
    gate = jax.nn.sigmoid(_dot(_rms(h, lnp_ref[...]).astype(BF16), wpg_ref[...]))
    return h + gate * _dot(p.astype(BF16), wpp_ref[...])


def _pool_mix(pin, tile_idx, pbuf, h, att, gsub_ref, wpool_ref, pscale_ref, wout_ref, sub_scale):
    tm = pin.shape[0]

    @pl.when(tile_idx == 0)
    def _():
        pbuf[0:POOL_HALO, :] = jnp.zeros((POOL_HALO, POOL_WIDTH), F32)

    pbuf[POOL_HALO:, :] = pin
    pos = tile_idx * tm + lax.broadcasted_iota(jnp.int32, (tm, 1), 0)
    d_groups = []
    for g, w in enumerate(POOL_WINDOWS):
        gs = slice(g * POOL_GROUP_DIM, (g + 1) * POOL_GROUP_DIM)
        tot = pin[:, gs]
        for back in range(1, w):
            tot = tot + pbuf[POOL_HALO - back:POOL_HALO - back + tm, gs]
        cnt = jnp.minimum(w, pos + 1).astype(F32)
        d_groups.append(tot / cnt - pin[:, gs])
    pbuf[0:POOL_HALO, :] = pbuf[tm:tm + POOL_HALO, :]
    pooled = _pool_project(d_groups, wpool_ref, pscale_ref)
    return _mix_residual(h, att, pooled, gsub_ref, wout_ref, sub_scale)


def _attn_kernel(pt_ref, lamp_ref, q_ref, k_ref, v_ref, h_ref, pin_ref, qs_ref, ks_ref, vs_ref, ck_hbm, cv_hbm,
                 gsub_ref, wpool_ref, pscale_ref, wout_ref,
                 h1_ref, os_ref, qq_scr, m_scr, l_scr, acc_scr, pbuf, *stream_scr,
                 blk, nq, lam_init, sub_scale, stream_cfg):
    i = pl.program_id(1)
    t = pl.program_id(0) * nq + i
    lam = _lambda(lamp_ref, lam_init)
    stream = _DecodeStream(pt_ref, ck_hbm, cv_hbm, stream_scr, **stream_cfg)

    @pl.when(t == 0)
    def _():
        stream.prime()

    g0 = stream.begin_step(t, qs_ref, ks_ref, vs_ref)

    lane = lax.broadcasted_iota(jnp.int32, (blk, V_DIM), 1)
    row = lax.broadcasted_iota(jnp.int32, (2 * blk, blk), 0)
    col = lax.broadcasted_iota(jnp.int32, (2 * blk, blk), 1)
    causal = col <= jnp.where(row >= blk, row - blk, row)
    reps = blk // LANES
    heads = [slice(h * V_DIM, (h + 1) * V_DIM) for h in range(N_HEADS)]

    for h, hs in enumerate(heads):
        qh = q_ref[0, :, hs]
        zero = jnp.zeros_like(qh)
        qq_scr[h, 0:blk, :] = jnp.where(lane < HEAD_DIM, qh, zero)
        qq_scr[h, blk:2 * blk, :] = jnp.where(lane >= HEAD_DIM, qh, zero)
    m_scr[...] = jnp.full(m_scr.shape, -jnp.inf, F32)
    l_scr[...] = jnp.zeros(l_scr.shape, F32)
    acc_scr[...] = jnp.zeros(acc_scr.shape, F32)

    def attn_scores(j, masked, which):
        start = pl.multiple_of(j * blk, blk)
        out = []
        for h in which:
            s = _dot_nt(qq_scr[h], k_ref[0, pl.ds(start, blk), heads[h]])
            out.append(jnp.where(causal, s, NEG) if masked else s)
        return out

    def attn_update(j, scores, which):
        start = pl.multiple_of(j * blk, blk)
        for s, h in zip(scores, which):
            hs = heads[h]
            m_prev = m_scr[h]
            m_new = jnp.maximum(m_prev, jnp.max(s, axis=1, keepdims=True))
            alpha = jnp.exp(m_prev - m_new)
            p = jnp.exp(s - jnp.concatenate([m_new] * reps, axis=1))
            l_scr[h] = alpha * l_scr[h] + jnp.sum(p, axis=1, keepdims=True)
            acc_scr[h] = alpha * acc_scr[h] + _dot(p.astype(BF16), v_ref[0, pl.ds(start, blk), hs])
            m_scr[h] = m_new

    stream.fetch(g0)
    s0 = stream.scores(g0)

    def body(j, s):
        stream.fetch(g0 + j + 1)
        half = N_HEADS // 2
        first = attn_scores(j, False, range(half))
        s_next = stream.scores(g0 + j + 1)
        attn_update(j, first, range(half))
        second = attn_scores(j, False, range(half, N_HEADS))
        stream.update(g0 + j, s)
        attn_update(j, second, range(half, N_HEADS))
        return s_next

    def rest(c, s):
        stream.fetch(g0 + c + 1)
        s_next = stream.scores(g0 + c + 1)
        stream.update(g0 + c, s)
        return s_next

    s_dec = lax.fori_loop(0, i, body, s0)
    attn_update(i, attn_scores(i, True, range(N_HEADS)), range(N_HEADS))
    s_dec = lax.fori_loop(i, stream.cpr - 1, rest, s_dec)
    stream.update(g0 + stream.cpr - 1, s_dec)
    att = []
    for h in range(N_HEADS):
        o = acc_scr[h] / l_scr[h]
        att.append(o[:blk] - lam * o[blk:])
    h1_ref[0] = _pool_mix(pin_ref[0], i, pbuf, h_ref[0], jnp.concatenate(att, axis=1),
                          gsub_ref, wpool_ref, pscale_ref, wout_ref, sub_scale)
    stream.finish_step(t, lam, os_ref)


def _prompt_attention(page_table, lamp, q, kb, vb, h, pin, qs, ks_new, vs_new, cache_k, cache_v,
                      mix_weights, lam_init, sub_scale, blk, first_req, ppc):
    b, s, _ = q.shape
    assert s % blk == 0
    nq = s // blk
    n_pages = page_table.shape[1]
    assert n_pages % ppc == 0 and nq <= n_pages // ppc
    stream_cfg = dict(first_req=first_req, n_steps=b * nq, n_pages=n_pages, ppc=ppc)
    kernel = functools.partial(_attn_kernel, blk=blk, nq=nq, lam_init=lam_init, sub_scale=sub_scale,
                               stream_cfg=stream_cfg)
    step_of = lambda bi, i: bi * nq + i
    os_spec, os_shape = _stream_out(step_of, b * nq)
    tile = lambda width: pl.BlockSpec((1, blk, width), lambda bi, i, pt: (bi, i, 0))
    resident = lambda arr: pl.BlockSpec(arr.shape, lambda bi, i, pt: (0,) * arr.ndim,
                                        pipeline_mode=pl.Buffered(1))
    grid_spec = pltpu.PrefetchScalarGridSpec(
        num_scalar_prefetch=1,
        grid=(b, nq),
        in_specs=[
            pl.BlockSpec((4, HEAD_DIM), lambda bi, i, pt: (0, 0)),
            tile(QK_WIDTH),
            pl.BlockSpec((1, s, QK_WIDTH), lambda bi, i, pt: (bi, 0, 0)),
            pl.BlockSpec((1, s, ATTN_WIDTH), lambda bi, i, pt: (bi, 0, 0)),
            tile(D_MODEL), tile(POOL_WIDTH),
        ] + _stream_specs(step_of, first_req) + [resident(w) for w in mix_weights],
        out_specs=(tile(D_MODEL), os_spec),
        scratch_shapes=[pltpu.VMEM((N_HEADS, 2 * blk, V_DIM), BF16)]
        + [pltpu.VMEM((N_HEADS, 2 * blk, V_DIM), F32)] * 3
        + [pltpu.VMEM((blk + POOL_HALO, POOL_WIDTH), F32)] + _stream_scratch(ppc),
    )
    return pl.pallas_call(
        kernel,
        grid_spec=grid_spec,
        out_shape=(jax.ShapeDtypeStruct((b, s, D_MODEL), F32), os_shape),
        compiler_params=pltpu.CompilerParams(
            dimension_semantics=("arbitrary", "arbitrary"), vmem_limit_bytes=VMEM_LIMIT_BYTES),
        name="prompt_attn",
    )(page_table.reshape(-1), lamp, q, kb, vb, h, pin, qs[:, None], ks_new[:, None], vs_new[:, None],
      cache_k, cache_v, *mix_weights)


def _post_prompt_kernel(pt_ref, lamp_ref, h1_ref, p_ref,
                        qs_ref, ks_ref, vs_ref, ck_hbm, cv_hbm,
                        ln2_ref, wug_ref, wuv_ref, cw_ref, cb_ref, wdn_ref, lnp_ref, wpg_ref, wpp_ref,
                        y_ref, ctail_ref, os_ref,
                        ubuf_g, ubuf_v, ucarry, a2_scr, act_scr, *stream_scr,
                        tm, n_tiles, lam_init, stream_cfg):
    t = pl.program_id(1)
    step = pl.program_id(0) * n_tiles + t
    stream = _DecodeStream(pt_ref, ck_hbm, cv_hbm, stream_scr, **stream_cfg)

    @pl.when(step == 0)
    def _():
        stream.prime()

    g0 = stream.begin_step(step, qs_ref, ks_ref, vs_ref)
    stream.fetch(g0)
    s_dec = stream.scores(g0)

    @pl.when(t == 0)
    def _():
        ucarry[...] = jnp.zeros(ucarry.shape, F32)

    h1 = h1_ref[0]

    a2_scr[...] = _rms(h1, ln2_ref[...]).astype(BF16)
    halves = ((wug_ref, ubuf_g), (wuv_ref, ubuf_v))

    def columns(half, c):
        return slice(half * FF_PAD + c * FF_CHUNK, half * FF_PAD + (c + 1) * FF_CHUNK)

    def ffn_up(c):
        for half, (w_ref, ubufs) in enumerate(halves):
            ubuf = ubufs.at[c % 2]
            ubuf[0:CONV_HALO, :] = ucarry[:, columns(half, c)]
            ubuf[CONV_HALO:, :] = _dot(a2_scr[...], w_ref[:, c * FF_CHUNK:(c + 1) * FF_CHUNK])

    def ffn_activate(c):
        conv = []
        for half, (_, ubufs) in enumerate(halves):
            ubuf = ubufs.at[c % 2]
            hs = columns(half, c)
            acc = cb_ref[:, hs]
            for j in range(CONV_W):
                lo = CONV_HALO - (CONV_W - 1) + j
                acc = acc + ubuf[lo:lo + tm, :] * cw_ref[j:j + 1, hs]
            ucarry[:, hs] = ubuf[tm:tm + CONV_HALO, :]
            conv.append(acc)
        act_scr[:, c * FF_CHUNK:(c + 1) * FF_CHUNK] = _gelu_gate(conv[0], conv[1]).astype(BF16)

    n_ffn = FF_PAD // FF_CHUNK
    ffn_up(0)
    for c in range(n_ffn):
        nxt = c + 1
        if nxt < stream.chunks_per_step:
            stream.fetch(g0 + nxt)
        if nxt < n_ffn:
            ffn_up(nxt)
        if nxt < stream.chunks_per_step:
            s_next = stream.scores(g0 + nxt)
        ffn_activate(c)
        if c < stream.chunks_per_step:
            stream.update(g0 + c, s_dec)
            s_dec = s_next
    ctail_ref[0] = ucarry[...]
    h2 = h1 + _dot(act_scr[...], wdn_ref[...])

    y_ref[0] = _ple(h2, p_ref[0], lnp_ref, wpg_ref, wpp_ref)
    stream.finish_step(step, _lambda(lamp_ref, lam_init), os_ref)


def _post_prompt(page_table, lamp, h1, p, qs, ks_new, vs_new, cache_k, cache_v, ffn_weights,
                 lam_init, tm, first_req, ppc, steps_per_req):
    b, s, _ = h1.shape
    assert s % tm == 0
    n_tiles = s // tm
    n_pages = page_table.shape[1]
    assert n_pages % ppc == 0 and n_pages // ppc <= FF_PAD // FF_CHUNK
    tile = lambda width: pl.BlockSpec((1, tm, width), lambda bi, t, pt: (bi, t, 0))
    resident = lambda arr: pl.BlockSpec(arr.shape, lambda bi, t, pt: (0,) * arr.ndim,
                                        pipeline_mode=pl.Buffered(1))
    stream_cfg = dict(first_req=first_req, n_steps=b * n_tiles, n_pages=n_pages, ppc=ppc,
                      steps_per_req=steps_per_req)
    kernel = functools.partial(_post_prompt_kernel, tm=tm, n_tiles=n_tiles, lam_init=lam_init,
                               stream_cfg=stream_cfg)
    step_of = lambda bi, t: bi * n_tiles + t
    os_spec, os_shape = _stream_out(step_of, b * n_tiles, steps_per_req)
    grid_spec = pltpu.PrefetchScalarGridSpec(
        num_scalar_prefetch=1,
        grid=(b, n_tiles),
        in_specs=[pl.BlockSpec((4, HEAD_DIM), lambda bi, t, pt: (0, 0)), tile(D_MODEL), tile(PLE_DIM)]
        + _stream_specs(step_of, first_req, steps_per_req) + [resident(w) for w in ffn_weights],
        out_specs=(tile(D_MODEL),
                   pl.BlockSpec((1, CONV_HALO, 2 * FF_PAD), lambda bi, t, pt: (bi, 0, 0)),
                   os_spec),
        scratch_shapes=[
            pltpu.VMEM((2, tm + CONV_HALO, FF_CHUNK), F32),
            pltpu.VMEM((2, tm + CONV_HALO, FF_CHUNK), F32),
            pltpu.VMEM((CONV_HALO, 2 * FF_PAD), F32),
            pltpu.VMEM((tm, D_MODEL), BF16),
            pltpu.VMEM((tm, FF_PAD), BF16),
        ] + _stream_scratch(ppc),
    )
    return pl.pallas_call(
        kernel,
        grid_spec=grid_spec,
        out_shape=(jax.ShapeDtypeStruct((b, s, D_MODEL), F32),
                   jax.ShapeDtypeStruct((b, CONV_HALO, 2 * FF_PAD), F32),
                   os_shape),
        compiler_params=pltpu.CompilerParams(
            dimension_semantics=("arbitrary", "arbitrary"), vmem_limit_bytes=VMEM_LIMIT_BYTES),
        name="post_prompt",
    )(page_table.reshape(-1), lamp, h1, p, qs[:, None], ks_new[:, None], vs_new[:, None],
      cache_k, cache_v, *ffn_weights)


def _post_sample_kernel(h_ref, att_ref, pin_ref, p_ref, spool_ref, sconv_ref,
                        gsub_ref, wpool_ref, pscale_ref, wout_ref, ln2_ref,
                        wug_ref, wuv_ref, cw_ref, cb_ref, wdn_ref, lnp_ref, wpg_ref, wpp_ref,
                        y_ref, u_ref, a2_scr, act_scr, *, sub_scale):
    pin = pin_ref[...]
    d_groups = []
    for g, w in enumerate(POOL_WINDOWS):
        gs = slice(g * POOL_GROUP_DIM, (g + 1) * POOL_GROUP_DIM)
        tot = pin[:, gs]
        for back in range(1, w):
            tot = tot + spool_ref[POOL_BUF - back, :, gs]
        d_groups.append(tot / float(w) - pin[:, gs])
    pooled = _pool_project(d_groups, wpool_ref, pscale_ref)

    h1 = _mix_residual(h_ref[...], att_ref[...], pooled, gsub_ref, wout_ref, sub_scale)

    a2_scr[...] = _rms(h1, ln2_ref[...]).astype(BF16)
    for c in range(FF_PAD // FF_CHUNK):
        cs = slice(c * FF_CHUNK, (c + 1) * FF_CHUNK)
        conv = []
        for half, w_ref in enumerate((wug_ref, wuv_ref)):
            hs = slice(half * FF_PAD + c * FF_CHUNK, half * FF_PAD + (c + 1) * FF_CHUNK)
            u = _dot(a2_scr[...], w_ref[:, cs])
            u_ref[:, hs] = u
            acc = cb_ref[:, hs] + u * cw_ref[CONV_W - 1:CONV_W, hs]
            for j in range(CONV_W - 1):
                acc = acc + sconv_ref[j, :, hs] * cw_ref[j:j + 1, hs]
            conv.append(acc)
        act_scr[:, cs] = _gelu_gate(conv[0], conv[1]).astype(BF16)
    h2 = h1 + _dot(act_scr[...], wdn_ref[...])

    y_ref[...] = _ple(h2, p_ref[...], lnp_ref, wpg_ref, wpp_ref)


def _post_sample(h, att, pin, p, spool, sconv, weights, sub_scale):
    n = h.shape[0]
    args = (h, att, pin, p, spool, sconv) + tuple(weights)
    full = lambda arr: pl.BlockSpec(arr.shape, lambda i: (0,) * arr.ndim, pipeline_mode=pl.Buffered(1))
    kernel = functools.partial(_post_sample_kernel, sub_scale=sub_scale)
    return pl.pallas_call(
        kernel,
        grid=(1,),
        in_specs=[full(a) for a in args],
        out_specs=(pl.BlockSpec((n, D_MODEL), lambda i: (0, 0)),
                   pl.BlockSpec((n, 2 * FF_PAD), lambda i: (0, 0))),
        out_shape=(jax.ShapeDtypeStruct((n, D_MODEL), F32),
                   jax.ShapeDtypeStruct((n, 2 * FF_PAD), F32)),
        scratch_shapes=[pltpu.VMEM((n, D_MODEL), BF16), pltpu.VMEM((n, FF_PAD), BF16)],
        compiler_params=pltpu.CompilerParams(
            dimension_semantics=("arbitrary",), vmem_limit_bytes=VMEM_LIMIT_BYTES),
        name="post_sample",
    )(*args)


def _split_ff(x):
    pad = [(0, 0)] * (x.ndim - 1) + [(0, FF_PAD - D_FF)]
    return jnp.concatenate([jnp.pad(x[..., :D_FF], pad), jnp.pad(x[..., D_FF:], pad)], axis=-1)


def _merge_ff(x):
    return jnp.concatenate([x[..., :D_FF], x[..., FF_PAD:FF_PAD + D_FF]], axis=-1)


def _layer(l, h_p, h_s, cache_k, cache_v, state_pool, state_conv, page_table, p_prompt, p_sample,
           ln1, w_in, g_q, g_k, lam_q1, lam_k1, lam_q2, lam_k2, g_sub, w_pool, pool_scale, w_out,
           ln2, w_up, conv_w, conv_b, w_down, ln_ple, w_pg, w_pp):
    b, s, _ = h_p.shape
    n_req = h_s.shape[0]
    n_pool = cache_k.shape[1]
    lam_init = 0.8 - 0.6 * math.exp(-0.3 * l)
    sub_scale = 1.0 - lam_init

    row = lambda x: x[l].reshape(1, -1)
    win_b = w_in[l].astype(BF16)
    gq = jnp.tile(g_q[l], QK_WIDTH // HEAD_DIM).reshape(1, -1) * (HEAD_DIM ** -0.5)
    gk = jnp.tile(g_k[l], QK_WIDTH // HEAD_DIM).reshape(1, -1)
    grp = jnp.arange(MXU_DIM) // HEAD_DIM
    gmat = jnp.where(grp[:, None] == grp[None, :], 1.0 / HEAD_DIM, 0.0).astype(BF16)
    lamp = jnp.stack([lam_q1[l], lam_k1[l], lam_q2[l], lam_k2[l]])
    wup_b = w_up[l].astype(BF16)
    ffpad = ((0, 0), (0, FF_PAD - D_FF))
    tail_weights = (
        row(g_sub), w_pool[l].astype(BF16), row(pool_scale), w_out[l].astype(BF16), row(ln2),
        jnp.pad(wup_b[:, :D_FF], ffpad), jnp.pad(wup_b[:, D_FF:], ffpad),
        _split_ff(conv_w[l]), _split_ff(conv_b[l].reshape(1, -1)),
        jnp.pad(w_down[l].astype(BF16), ((0, FF_PAD - D_FF), (0, 0))),
        row(ln_ple), w_pg[l].astype(BF16), w_pp[l].astype(BF16),
    )

    qs, kts, v4s, pins, kbs, vbs = _project(h_s.reshape(n_req, D_MODEL), row(ln1), win_b, gmat, gq, gk,
                                            tm=n_req, seq=n_req)
    ck_view = jnp.transpose(cache_k[l], (0, 2, 3, 4, 1)).reshape(n_pool, QK_WIDTH, PAGE_SIZE)
    cv_view = cache_v[l].reshape(n_pool, PAGE_ROWS, V_DIM)
    stream_args = (qs, kbs, vbs, ck_view, cv_view)

    shp = lambda x: x.reshape(b, s, x.shape[-1])
    n_proj_req = b * s // PROJ_TILE
    n_attn_req = b * s // TOKEN_TILE
    n_post_req = b * s // TOKEN_TILE // POST_STEPS_PER_REQ
    assert n_req == n_proj_req + n_attn_req + n_post_req, "decode requests must fill the host kernels' steps"
    q, kt, v4, pin, kb, vb, att_s0 = _project(
        h_p.reshape(b * s, D_MODEL), row(ln1), win_b, gmat, gq, gk, tm=PROJ_TILE, seq=s,
        host=(page_table, lamp, *stream_args, lam_init, 0, STREAM_PAGES))
    n_mix = 4
    h1, att_s1 = _prompt_attention(page_table, lamp, shp(q), shp(kb), shp(vb), h_p, shp(pin), *stream_args,
                                   tail_weights[:n_mix], lam_init, sub_scale, blk=TOKEN_TILE,
                                   first_req=n_proj_req, ppc=STREAM_PAGES)
    y_p, ctail, att_s2 = _post_prompt(page_table, lamp, h1, p_prompt[l], *stream_args,
                                      tail_weights[n_mix:], lam_init, tm=TOKEN_TILE,
                                      first_req=n_proj_req + n_attn_req, ppc=STREAM_PAGES,
                                      steps_per_req=POST_STEPS_PER_REQ)
    untranspose_k = lambda x, n: jnp.transpose(x.reshape(-1, N_HEADS, 2, HEAD_DIM, n), (0, 4, 1, 2, 3))
    k_p = untranspose_k(kt, s)
    v_p = v4.reshape(b, s, N_HEADS, V_DIM)
    pool_p = shp(pin)[:, s - POOL_BUF:]
    conv_p = _merge_ff(ctail[:, CONV_HALO - (CONV_W - 1):])

    att_s = jnp.concatenate([att_s0[:, 0], att_s1[:, 0], att_s2[:, 0]], axis=0)
    sconv = jnp.moveaxis(_split_ff(state_conv[l]), 1, 0)
    y_s, u_s = _post_sample(h_s.reshape(n_req, D_MODEL), att_s, pins, p_sample[l].reshape(n_req, PLE_DIM),
                            jnp.moveaxis(state_pool[l], 1, 0), sconv,
                            tail_weights, sub_scale)
    k_s = untranspose_k(kts, n_req).reshape(n_req, 1, N_HEADS, 2, HEAD_DIM)
    v_s = v4s.reshape(n_req, 1, N_HEADS, V_DIM)
    pool_s = jnp.concatenate([state_pool[l][:, 1:], pins[:, None]], axis=1)
    conv_s = jnp.concatenate([state_conv[l][:, 1:], _merge_ff(u_s)[:, None]], axis=1)

    return (y_p, y_s.reshape(n_req, 1, D_MODEL)), (k_p, v_p, pool_p, conv_p, k_s, v_s, pool_s, conv_s)


def kernel(x_prompt, x_sample, cache_k, cache_v, state_pool, state_conv, page_table, p_prompt, p_sample, ln1, w_in, g_q, g_k, lam_q1, lam_k1, lam_q2, lam_k2, g_sub, w_pool, pool_scale, w_out, ln2, w_up, conv_w, conv_b, w_down, ln_ple, w_pg, w_pp):
    depth = ln1.shape[0]
    h_p, h_s = x_prompt, x_sample
    per_layer = []
    for l in range(depth):
        (h_p, h_s), outs = _layer(
            l, h_p, h_s, cache_k, cache_v, state_pool, state_conv, page_table, p_prompt, p_sample,
            ln1, w_in, g_q, g_k, lam_q1, lam_k1, lam_q2, lam_k2, g_sub, w_pool, pool_scale, w_out,
            ln2, w_up, conv_w, conv_b, w_down, ln_ple, w_pg, w_pp)
        per_layer.append(outs)
    stacked = tuple(jnp.stack(xs) for xs in zip(*per_layer))
    return (h_p, h_s) + stacked
```

```python
import functools
import math

import jax
import jax.numpy as jnp
from jax import lax
from jax.experimental import pallas as pl
from jax.experimental.pallas import tpu as pltpu

F32 = jnp.float32
BF16 = jnp.bfloat16

D_MODEL = 1024
N_HEADS = 4
HEAD_DIM = 64
V_DIM = 2 * HEAD_DIM
QK_WIDTH = N_HEADS * 2 * HEAD_DIM
ATTN_WIDTH = N_HEADS * V_DIM
POOL_WINDOWS = (2, 4, 8, 16)
POOL_WIDTH = D_MODEL - ATTN_WIDTH
POOL_GROUP_DIM = POOL_WIDTH // len(POOL_WINDOWS)
POOL_BUF = max(POOL_WINDOWS) - 1
IN_WIDTH = 2 * QK_WIDTH + ATTN_WIDTH + POOL_WIDTH
D_FF = 2752
CONV_W = 3
PLE_DIM = 256
PAGE_SIZE = 128
PAGE_ROWS = PAGE_SIZE * N_HEADS
EPS = 1e-6
NEG = -1e30

LANES = 128
SUBLANES = 8
MXU_DIM = 256
N_DMA_PRIORITIES = 2
STREAM_AHEAD = 3
STREAM_SLOTS = STREAM_AHEAD + 2
VMEM_LIMIT_BYTES = 60 * 1024 * 1024

FF_PAD = -(-D_FF // MXU_DIM) * MXU_DIM
FF_CHUNK = MXU_DIM
TOKEN_TILE = 256
PROJ_TILE = 512
HOST_SHARE = (2, 3, 3)
STREAM_PAGES = 8
POOL_HALO = 16
CONV_HALO = SUBLANES


def _rms(x, g):
    ms = jnp.mean(x * x, axis=-1, keepdims=True)
    return x * lax.rsqrt(ms + EPS) * g


def _dot(a, b):
    return jnp.dot(a, b, preferred_element_type=F32)


def _dot_nt(a, b):
    return lax.dot_general(a, b, (((1,), (1,)), ((), ())), preferred_element_type=F32)


def _lambda(lamp_ref, lam_init):
    p = lamp_ref[...]
    s1 = jnp.sum(p[0:1] * p[1:2], axis=-1, keepdims=True)
    s2 = jnp.sum(p[2:3] * p[3:4], axis=-1, keepdims=True)
    return jnp.exp(s1) - jnp.exp(s2) + lam_init


def _proj_kernel(*refs, stream_cfg, lam_init):
    if stream_cfg is None:
        (h_ref, ln1_ref, win_ref, gmat_ref, gq_ref, gk_ref,
         q_ref, k_ref, v_ref, pin_ref, kb_ref, vb_ref, a_scr) = refs
        stream = None
    else:
        (pt_ref, h_ref, ln1_ref, win_ref, gmat_ref, gq_ref, gk_ref, lamp_ref,
         qs_ref, ks_ref, vs_ref, ck_hbm, cv_hbm,
         q_ref, k_ref, v_ref, pin_ref, kb_ref, vb_ref, os_ref, a_scr, *stream_scr) = refs
        step = pl.program_id(0)
        stream = _DecodeStream(pt_ref, (qs_ref, ks_ref, vs_ref), (ck_hbm, cv_hbm), os_ref, stream_scr,
                               _lambda(lamp_ref, lam_init), **stream_cfg)

        @pl.when(step == 0)
        def _():
            stream.prime()

        g0 = step * stream.chunks_per_step
        stream.fetch(g0)
        s_dec = stream.scores(g0)

    tm = h_ref.shape[0]
    a_scr[...] = _rms(h_ref[...], ln1_ref[...]).astype(BF16)
    gmat = gmat_ref[...]
    n_chunks = IN_WIDTH // MXU_DIM
    per_seg = QK_WIDTH // MXU_DIM
    for c in range(n_chunks):
        hosted = stream is not None and c < stream.chunks_per_step
        prefetch = stream is not None and c + 1 < stream.chunks_per_step
        if prefetch:
            stream.fetch(g0 + c + 1)
        z = _dot(a_scr[...], win_ref[:, c * MXU_DIM:(c + 1) * MXU_DIM])
        seg, off = divmod(c, per_seg)
        sl = slice(off * MXU_DIM, (off + 1) * MXU_DIM)
        if seg < 2:
            ms = _dot((z * z).astype(BF16), gmat)
            y = z * lax.rsqrt(ms + EPS)
            if seg == 0:
                q_ref[:, sl] = (y * gq_ref[:, sl]).astype(BF16)
            else:
                y = y * gk_ref[:, sl]
                k_ref[0, sl, :] = y.T
                kb_ref[:, sl] = y.astype(BF16)
        elif seg == 2:
            for hh in range(MXU_DIM // V_DIM):
                head = off * (MXU_DIM // V_DIM) + hh
                v_ref[pl.ds(head, tm, stride=N_HEADS), :] = z[:, hh * V_DIM:(hh + 1) * V_DIM]
            vb_ref[:, sl] = z.astype(BF16)
        else:
            pin_ref[:, sl] = z
        if prefetch:
            s_next = stream.scores(g0 + c + 1)
        if hosted:
            stream.update(g0 + c, s_dec)
            s_dec = s_next if prefetch else None


def _project(h2d, ln1, win_b, gmat, gq, gk, tm, seq, host=None):
    t = h2d.shape[0]
    assert t % seq == 0 and seq % tm == 0
    tiles = seq // tm
    n_steps = t // tm
    row = lambda i, *_: (i, 0)
    const = lambda i, *_: (0, 0)
    w512 = pl.BlockSpec((tm, QK_WIDTH), row)
    out_shape = [
        jax.ShapeDtypeStruct((t, QK_WIDTH), BF16),
        jax.ShapeDtypeStruct((t // seq, QK_WIDTH, seq), F32),
        jax.ShapeDtypeStruct((t * N_HEADS, V_DIM), F32),
        jax.ShapeDtypeStruct((t, POOL_WIDTH), F32),
        jax.ShapeDtypeStruct((t, QK_WIDTH), BF16),
        jax.ShapeDtypeStruct((t, ATTN_WIDTH), BF16),
    ]
    out_specs = [
        w512,
        pl.BlockSpec((1, QK_WIDTH, tm), lambda i, *_: (i // tiles, 0, i % tiles)),
        pl.BlockSpec((tm * N_HEADS, V_DIM), row),
        w512, w512, w512,
    ]
    in_specs = [
        pl.BlockSpec((tm, D_MODEL), row),
        pl.BlockSpec((1, D_MODEL), const),
        pl.BlockSpec((D_MODEL, IN_WIDTH), const, pipeline_mode=pl.Buffered(1)),
        pl.BlockSpec((MXU_DIM, MXU_DIM), const),
        pl.BlockSpec((1, QK_WIDTH), const),
        pl.BlockSpec((1, QK_WIDTH), const),
    ]
    scratch = [pltpu.VMEM((tm, D_MODEL), BF16)]
    args = (h2d, ln1, win_b, gmat, gq, gk)
    params = pltpu.CompilerParams(dimension_semantics=("arbitrary",), vmem_limit_bytes=VMEM_LIMIT_BYTES)
    if host is None:
        kernel = functools.partial(_proj_kernel, stream_cfg=None, lam_init=None)
        return pl.pallas_call(
            kernel, grid=(n_steps,), in_specs=in_specs, out_specs=out_specs, out_shape=out_shape,
            scratch_shapes=scratch, compiler_params=params, name="proj",
        )(*args)

    page_table, lamp, qs, ks_new, vs_new, cache_k, cache_v, lam_init, first_req, n_host_req, ppc = host
    stream_cfg = _stream_cfg(first_req, n_host_req, n_steps, page_table.shape[1], ppc)
    assert stream_cfg["chunks_per_step"] <= IN_WIDTH // MXU_DIM
    os_spec, os_shape = _stream_out(n_host_req)
    grid_spec = pltpu.PrefetchScalarGridSpec(
        num_scalar_prefetch=1,
        grid=(n_steps,),
        in_specs=in_specs + [pl.BlockSpec((4, HEAD_DIM), const)] + _stream_specs(qs.shape[0]),
        out_specs=out_specs + [os_spec],
        scratch_shapes=scratch + _stream_scratch(ppc),
    )
    kernel = functools.partial(_proj_kernel, stream_cfg=stream_cfg, lam_init=lam_init)
    return pl.pallas_call(
        kernel, grid_spec=grid_spec, out_shape=out_shape + [os_shape], compiler_params=params,
        name="proj_host",
    )(page_table.reshape(-1), *args, lamp, qs, ks_new, vs_new, cache_k, cache_v)


def _stream_scratch(ppc):
    return [
        pltpu.VMEM((STREAM_SLOTS, ppc, QK_WIDTH, PAGE_SIZE), F32),
        pltpu.VMEM((STREAM_SLOTS, ppc * PAGE_ROWS, V_DIM), F32),
        pltpu.SemaphoreType.DMA((STREAM_SLOTS, 2)),
        pltpu.VMEM((2, 2 * N_HEADS, QK_WIDTH), BF16),
        pltpu.VMEM((2 * N_HEADS, LANES), F32),
        pltpu.VMEM((2 * N_HEADS, LANES), F32),
        pltpu.VMEM((2 * N_HEADS, ATTN_WIDTH), F32),
    ]


class _DecodeStream:
    def __init__(self, pt_ref, rows, caches, out_ref, scratch, lam, *,
                 first_req, n_steps, n_pages, ppc, chunks_per_step):
        self.pt_ref, self.out_ref, self.lam = pt_ref, out_ref, lam
        self.q_ref, self.k_ref, self.v_ref = rows
        self.ck_hbm, self.cv_hbm = caches
        (self.kbuf, self.vbuf, self.sem, self.qbd, self.m, self.l, self.acc) = scratch
        self.first_req, self.n_pages, self.ppc = first_req, n_pages, ppc
        self.cpr = n_pages // ppc
        self.chunks_per_step = chunks_per_step
        self.total = n_steps * chunks_per_step
        assert self.total % self.cpr == 0
        self.hc = lax.broadcasted_iota(jnp.int32, (2 * N_HEADS, QK_WIDTH), 0)

    def _request(self, g):
        return self.first_req + lax.div(g, self.cpr)

    def _copies(self, g, slot, real):
        base = self._request(g) * self.n_pages + lax.rem(g, self.cpr) * self.ppc
        copies = []
        for p in range(self.ppc):
            page = self.pt_ref[base + p] if real else 0
            rows = pl.ds(p * PAGE_ROWS, PAGE_ROWS)
            copies.append(pltpu.make_async_copy(self.ck_hbm.at[page], self.kbuf.at[slot, p],
                                                self.sem.at[slot, 0]))
            copies.append(pltpu.make_async_copy(self.cv_hbm.at[page], self.vbuf.at[slot, rows],
                                                self.sem.at[slot, 1]))
        return copies

    def _start(self, g):
        for n, cp in enumerate(self._copies(g, lax.rem(g, STREAM_SLOTS), True)):
            cp.start(priority=n % N_DMA_PRIORITIES)

    def prime(self):
        for g in range(STREAM_AHEAD):
            self._start(jnp.int32(g))
        for ref in (self.m, self.l, self.acc):
            ref[...] = jnp.zeros(ref.shape, F32)

    def fetch(self, g):
        ahead = g + STREAM_AHEAD

        @pl.when(ahead < self.total)
        def _():
            self._start(ahead)

        for cp in self._copies(g, lax.rem(g, STREAM_SLOTS), False):
            cp.wait()

    def scores(self, g):
        req = self._request(g)
        parity = lax.rem(req, 2)
        grp = lax.broadcasted_iota(jnp.int32, self.hc.shape, 1) // HEAD_DIM
        q_row = self.q_ref[pl.ds(req, 1), :]
        qbd = jnp.where(self.hc == grp, jnp.broadcast_to(q_row, self.hc.shape), 0.0).astype(BF16)
        self.qbd[parity] = qbd
        slot = lax.rem(g, STREAM_SLOTS)
        return jnp.concatenate(
            [_dot(qbd, self.kbuf[slot, p].astype(BF16)) for p in range(self.ppc)], axis=1)

    def update(self, g, s):
        req = self._request(g)
        first = lax.rem(g, self.cpr) == 0
        qbd = self.qbd[lax.rem(req, 2)].astype(F32)
        s_self = jnp.sum(qbd * self.k_ref[pl.ds(req, 1), :], axis=-1, keepdims=True)
        m_prev = jnp.where(first, jnp.broadcast_to(s_self, self.m.shape), self.m[...])
        l_prev = jnp.where(first, 1.0, self.l[...])
        acc_prev = jnp.where(first, jnp.broadcast_to(self.v_ref[pl.ds(req, 1), :], self.acc.shape), self.acc[...])

        slot = lax.rem(g, STREAM_SLOTS)
        n_tok = self.ppc * PAGE_SIZE
        m_new = jnp.maximum(m_prev, jnp.max(s, axis=1, keepdims=True))
        alpha = jnp.exp(m_prev - m_new)
        p = jnp.exp(s - m_new[:, :1])
        l_new = alpha * l_prev + jnp.sum(p, axis=1, keepdims=True)
        pb = p.astype(BF16)
        pv = [_dot(pb, self.vbuf[slot, pl.ds(h, n_tok, stride=N_HEADS), :].astype(BF16))
              for h in range(N_HEADS)]
        acc_new = alpha[:, :1] * acc_prev + jnp.concatenate(pv, axis=1)
        self.l[...] = l_new
        self.acc[...] = acc_new
        self.m[...] = m_new

        coef = jnp.where(lax.rem(self.hc, 2) == 0, 1.0, -self.lam)
        head_of_lane = lax.broadcasted_iota(jnp.int32, self.hc.shape, 1) // V_DIM
        keep = head_of_lane == self.hc // 2
        o = jnp.where(keep, coef * acc_new / l_new[:, :1], 0.0)
        row = jnp.sum(o, axis=0, keepdims=True)
        self.out_ref[req - self.first_req] = jnp.broadcast_to(row, self.out_ref.shape[1:])


def _stream_specs(n_req):
    const = lambda *ids: (0, 0)
    rows = pl.BlockSpec((n_req, QK_WIDTH), const)
    return [rows, rows, rows, pl.BlockSpec(memory_space=pl.ANY), pl.BlockSpec(memory_space=pl.ANY)]


def _stream_out(n_host_req):
    return (pl.BlockSpec((n_host_req, SUBLANES, ATTN_WIDTH), lambda *ids: (0, 0, 0)),
            jax.ShapeDtypeStruct((n_host_req, SUBLANES, ATTN_WIDTH), F32))


def _stream_cfg(first_req, n_host_req, n_steps, n_pages, ppc):
    chunks = n_host_req * (n_pages // ppc)
    assert n_pages % ppc == 0 and chunks % n_steps == 0, "the hosted requests must fill the grid evenly"
    return dict(first_req=first_req, n_steps=n_steps, n_pages=n_pages, ppc=ppc,
                chunks_per_step=chunks // n_steps)


def _mix_residual(h, att, pooled, gsub_ref, wout_ref, sub_scale):
    parts = [_rms(att[:, hd * V_DIM:(hd + 1) * V_DIM], gsub_ref[...]) * sub_scale for hd in range(N_HEADS)]
    mixed = jnp.concatenate(parts + [pooled], axis=1).astype(BF16)
    return h + _dot(mixed, wout_ref[...])


def _pool_project(d_groups, wpool_ref, pscale_ref):
    ys = [_dot(d.astype(BF16), wpool_ref[g]) for g, d in enumerate(d_groups)]
    return jnp.concatenate(ys, axis=1) * pscale_ref[...]


def _gelu_gate(g, val):
    return 0.5 * g * (1.0 + lax.erf(g * (1.0 / math.sqrt(2.0)))) * val


def _ple(h, p, lnp_ref, wpg_ref, wpp_ref):
    gate = jax.nn.sigmoid(_dot(_rms(h, lnp_ref[...]).astype(BF16), wpg_ref[...]))
    return h + gate * _dot(p.astype(BF16), wpp_ref[...])


def _pool_mix(pin, tile_idx, pbuf, h, att, gsub_ref, wpool_ref, pscale_ref, wout_ref, sub_scale):
    tm = pin.shape[0]

    @pl.when(tile_idx == 0)
    def _():
        pbuf[0:POOL_HALO, :] = jnp.zeros((POOL_HALO, POOL_WIDTH), F32)

    pbuf[POOL_HALO:, :] = pin
    pos = tile_idx * tm + lax.broadcasted_iota(jnp.int32, (tm, 1), 0)
    d_groups = []
    for g, w in enumerate(POOL_WINDOWS):
        gs = slice(g * POOL_GROUP_DIM, (g + 1) * POOL_GROUP_DIM)
        tot = pin[:, gs]
        for back in range(1, w):
            tot = tot + pbuf[POOL_HALO - back:POOL_HALO - back + tm, gs]
        cnt = jnp.minimum(w, pos + 1).astype(F32)
        d_groups.append(tot / cnt - pin[:, gs])
    pbuf[0:POOL_HALO, :] = pbuf[tm:tm + POOL_HALO, :]
    pooled = _pool_project(d_groups, wpool_ref, pscale_ref)
    return _mix_residual(h, att, pooled, gsub_ref, wout_ref, sub_scale)


def _attn_kernel(pt_ref, lamp_ref, q_ref, k_ref, v_ref, h_ref, pin_ref, qs_ref, ks_ref, vs_ref, ck_hbm, cv_hbm,
                 gsub_ref, wpool_ref, pscale_ref, wout_ref,
                 h1_ref, os_ref, qq_scr, m_scr, l_scr, acc_scr, pbuf, *stream_scr,
                 blk, nq, lam_init, sub_scale, stream_cfg):
    i = pl.program_id(1)
    t = pl.program_id(0) * nq + i
    lam = _lambda(lamp_ref, lam_init)
    stream = _DecodeStream(pt_ref, (qs_ref, ks_ref, vs_ref), (ck_hbm, cv_hbm), os_ref, stream_scr, lam,
                           **stream_cfg)

    @pl.when(t == 0)
    def _():
        stream.prime()

    g0 = t * stream.chunks_per_step

    lane = lax.broadcasted_iota(jnp.int32, (blk, V_DIM), 1)
    row = lax.broadcasted_iota(jnp.int32, (2 * blk, blk), 0)
    col = lax.broadcasted_iota(jnp.int32, (2 * blk, blk), 1)
    causal = col <= jnp.where(row >= blk, row - blk, row)
    reps = blk // LANES
    heads = [slice(h * V_DIM, (h + 1) * V_DIM) for h in range(N_HEADS)]

    for h, hs in enumerate(heads):
        qh = q_ref[0, :, hs]
        zero = jnp.zeros_like(qh)
        qq_scr[h, 0:blk, :] = jnp.where(lane < HEAD_DIM, qh, zero)
        qq_scr[h, blk:2 * blk, :] = jnp.where(lane >= HEAD_DIM, qh, zero)
    m_scr[...] = jnp.full(m_scr.shape, -jnp.inf, F32)
    l_scr[...] = jnp.zeros(l_scr.shape, F32)
    acc_scr[...] = jnp.zeros(acc_scr.shape, F32)

    def attn_scores(j, masked, which):
        start = pl.multiple_of(j * blk, blk)
        out = []
        for h in which:
            s = _dot_nt(qq_scr[h], k_ref[0, pl.ds(start, blk), heads[h]])
            out.append(jnp.where(causal, s, NEG) if masked else s)
        return out

    def attn_update(j, scores, which):
        start = pl.multiple_of(j * blk, blk)
        for s, h in zip(scores, which):
            hs = heads[h]
            m_prev = m_scr[h]
            m_new = jnp.maximum(m_prev, jnp.max(s, axis=1, keepdims=True))
            alpha = jnp.exp(m_prev - m_new)
            p = jnp.exp(s - jnp.concatenate([m_new] * reps, axis=1))
            l_scr[h] = alpha * l_scr[h] + jnp.sum(p, axis=1, keepdims=True)
            acc_scr[h] = alpha * acc_scr[h] + _dot(p.astype(BF16), v_ref[0, pl.ds(start, blk), hs])
            m_scr[h] = m_new

    stream.fetch(g0)
    s0 = stream.scores(g0)

    def prompt_block(j, between=lambda: None, after=lambda: None):
        half = N_HEADS // 2
        first = attn_scores(j, False, range(half))
        between()
        attn_update(j, first, range(half))
        second = attn_scores(j, False, range(half, N_HEADS))
        after()
        attn_update(j, second, range(half, N_HEADS))

    def body(j, s):
        stream.fetch(g0 + j + 1)
        nxt = []
        prompt_block(j, lambda: nxt.append(stream.scores(g0 + j + 1)), lambda: stream.update(g0 + j, s))
        return nxt[0]

    def plain(j, carry):
        prompt_block(j)
        return carry

    def rest(c, s):
        stream.fetch(g0 + c + 1)
        s_next = stream.scores(g0 + c + 1)
        stream.update(g0 + c, s)
        return s_next

    last = stream.chunks_per_step - 1
    n_hosted = jnp.minimum(i, last)
    s_dec = lax.fori_loop(0, n_hosted, body, s0)
    lax.fori_loop(n_hosted, i, plain, 0)
    attn_update(i, attn_scores(i, True, range(N_HEADS)), range(N_HEADS))
    s_dec = lax.fori_loop(n_hosted, last, rest, s_dec)
    stream.update(g0 + last, s_dec)
    att = []
    for h in range(N_HEADS):
        o = acc_scr[h] / l_scr[h]
        att.append(o[:blk] - lam * o[blk:])
    h1_ref[0] = _pool_mix(pin_ref[0], i, pbuf, h_ref[0], jnp.concatenate(att, axis=1),
                          gsub_ref, wpool_ref, pscale_ref, wout_ref, sub_scale)


def _prompt_attention(page_table, lamp, q, kb, vb, h, pin, qs, ks_new, vs_new, cache_k, cache_v,
                      mix_weights, lam_init, sub_scale, blk, first_req, n_host_req, ppc):
    b, s, _ = q.shape
    assert s % blk == 0
    nq = s // blk
    stream_cfg = _stream_cfg(first_req, n_host_req, b * nq, page_table.shape[1], ppc)
    kernel = functools.partial(_attn_kernel, blk=blk, nq=nq, lam_init=lam_init, sub_scale=sub_scale,
                               stream_cfg=stream_cfg)
    os_spec, os_shape = _stream_out(n_host_req)
    tile = lambda width: pl.BlockSpec((1, blk, width), lambda bi, i, pt: (bi, i, 0))
    resident = lambda arr: pl.BlockSpec(arr.shape, lambda bi, i, pt: (0,) * arr.ndim,
                                        pipeline_mode=pl.Buffered(1))
    grid_spec = pltpu.PrefetchScalarGridSpec(
        num_scalar_prefetch=1,
        grid=(b, nq),
        in_specs=[
            pl.BlockSpec((4, HEAD_DIM), lambda bi, i, pt: (0, 0)),
            tile(QK_WIDTH),
            pl.BlockSpec((1, s, QK_WIDTH), lambda bi, i, pt: (bi, 0, 0)),
            pl.BlockSpec((1, s, ATTN_WIDTH), lambda bi, i, pt: (bi, 0, 0)),
            tile(D_MODEL), tile(POOL_WIDTH),
        ] + _stream_specs(qs.shape[0]) + [resident(w) for w in mix_weights],
        out_specs=(tile(D_MODEL), os_spec),
        scratch_shapes=[pltpu.VMEM((N_HEADS, 2 * blk, V_DIM), BF16)]
        + [pltpu.VMEM((N_HEADS, 2 * blk, V_DIM), F32)] * 3
        + [pltpu.VMEM((blk + POOL_HALO, POOL_WIDTH), F32)] + _stream_scratch(ppc),
    )
    return pl.pallas_call(
        kernel,
        grid_spec=grid_spec,
        out_shape=(jax.ShapeDtypeStruct((b, s, D_MODEL), F32), os_shape),
        compiler_params=pltpu.CompilerParams(
            dimension_semantics=("arbitrary", "arbitrary"), vmem_limit_bytes=VMEM_LIMIT_BYTES),
        name="prompt_attn",
    )(page_table.reshape(-1), lamp, q, kb, vb, h, pin, qs, ks_new, vs_new, cache_k, cache_v, *mix_weights)


def _post_prompt_kernel(pt_ref, lamp_ref, h1_ref, p_ref,
                        qs_ref, ks_ref, vs_ref, ck_hbm, cv_hbm,
                        ln2_ref, wug_ref, wuv_ref, cw_ref, cb_ref, wdn_ref, lnp_ref, wpg_ref, wpp_ref,
                        y_ref, ctail_ref, os_ref,
                        ubuf_g, ubuf_v, ucarry, a2_scr, act_scr, *stream_scr,
                        tm, n_tiles, lam_init, stream_cfg):
    t = pl.program_id(1)
    step = pl.program_id(0) * n_tiles + t
    stream = _DecodeStream(pt_ref, (qs_ref, ks_ref, vs_ref), (ck_hbm, cv_hbm), os_ref, stream_scr,
                           _lambda(lamp_ref, lam_init), **stream_cfg)

    @pl.when(step == 0)
    def _():
        stream.prime()

    g0 = step * stream.chunks_per_step
    stream.fetch(g0)
    s_dec = stream.scores(g0)

    @pl.when(t == 0)
    def _():
        ucarry[...] = jnp.zeros(ucarry.shape, F32)

    h1 = h1_ref[0]

    a2_scr[...] = _rms(h1, ln2_ref[...]).astype(BF16)
    halves = ((wug_ref, ubuf_g), (wuv_ref, ubuf_v))

    def columns(half, c):
        return slice(half * FF_PAD + c * FF_CHUNK, half * FF_PAD + (c + 1) * FF_CHUNK)

    def ffn_up(c):
        for half, (w_ref, ubufs) in enumerate(halves):
            ubuf = ubufs.at[c % 2]
            ubuf[0:CONV_HALO, :] = ucarry[:, columns(half, c)]
            ubuf[CONV_HALO:, :] = _dot(a2_scr[...], w_ref[:, c * FF_CHUNK:(c + 1) * FF_CHUNK])

    def ffn_activate(c):
        conv = []
        for half, (_, ubufs) in enumerate(halves):
            ubuf = ubufs.at[c % 2]
            hs = columns(half, c)
            acc = cb_ref[:, hs]
            for j in range(CONV_W):
                lo = CONV_HALO - (CONV_W - 1) + j
                acc = acc + ubuf[lo:lo + tm, :] * cw_ref[j:j + 1, hs]
            ucarry[:, hs] = ubuf[tm:tm + CONV_HALO, :]
            conv.append(acc)
        act_scr[:, c * FF_CHUNK:(c + 1) * FF_CHUNK] = _gelu_gate(conv[0], conv[1]).astype(BF16)

    n_ffn = FF_PAD // FF_CHUNK
    ffn_up(0)
    for c in range(n_ffn):
        nxt = c + 1
        if nxt < stream.chunks_per_step:
            stream.fetch(g0 + nxt)
        if nxt < n_ffn:
            ffn_up(nxt)
        if nxt < stream.chunks_per_step:
            s_next = stream.scores(g0 + nxt)
        ffn_activate(c)
        if c < stream.chunks_per_step:
            stream.update(g0 + c, s_dec)
            s_dec = s_next
    ctail_ref[0] = ucarry[...]
    h2 = h1 + _dot(act_scr[...], wdn_ref[...])

    y_ref[0] = _ple(h2, p_ref[0], lnp_ref, wpg_ref, wpp_ref)


def _post_prompt(page_table, lamp, h1, p, qs, ks_new, vs_new, cache_k, cache_v, ffn_weights,
                 lam_init, tm, first_req, n_host_req, ppc):
    b, s, _ = h1.shape
    assert s % tm == 0
    n_tiles = s // tm
    tile = lambda width: pl.BlockSpec((1, tm, width), lambda bi, t, pt: (bi, t, 0))
    resident = lambda arr: pl.BlockSpec(arr.shape, lambda bi, t, pt: (0,) * arr.ndim,
                                        pipeline_mode=pl.Buffered(1))
    stream_cfg = _stream_cfg(first_req, n_host_req, b * n_tiles, page_table.shape[1], ppc)
    assert stream_cfg["chunks_per_step"] <= FF_PAD // FF_CHUNK
    kernel = functools.partial(_post_prompt_kernel, tm=tm, n_tiles=n_tiles, lam_init=lam_init,
                               stream_cfg=stream_cfg)
    os_spec, os_shape = _stream_out(n_host_req)
    grid_spec = pltpu.PrefetchScalarGridSpec(
        num_scalar_prefetch=1,
        grid=(b, n_tiles),
        in_specs=[pl.BlockSpec((4, HEAD_DIM), lambda bi, t, pt: (0, 0)), tile(D_MODEL), tile(PLE_DIM)]
        + _stream_specs(qs.shape[0]) + [resident(w) for w in ffn_weights],
        out_specs=(tile(D_MODEL),
                   pl.BlockSpec((1, CONV_HALO, 2 * FF_PAD), lambda bi, t, pt: (bi, 0, 0)),
                   os_spec),
        scratch_shapes=[
            pltpu.VMEM((2, tm + CONV_HALO, FF_CHUNK), F32),
            pltpu.VMEM((2, tm + CONV_HALO, FF_CHUNK), F32),
            pltpu.VMEM((CONV_HALO, 2 * FF_PAD), F32),
            pltpu.VMEM((tm, D_MODEL), BF16),
            pltpu.VMEM((tm, FF_PAD), BF16),
        ] + _stream_scratch(ppc),
    )
    return pl.pallas_call(
        kernel,
        grid_spec=grid_spec,
        out_shape=(jax.ShapeDtypeStruct((b, s, D_MODEL), F32),
                   jax.ShapeDtypeStruct((b, CONV_HALO, 2 * FF_PAD), F32),
                   os_shape),
        compiler_params=pltpu.CompilerParams(
            dimension_semantics=("arbitrary", "arbitrary"), vmem_limit_bytes=VMEM_LIMIT_BYTES),
        name="post_prompt",
    )(page_table.reshape(-1), lamp, h1, p, qs, ks_new, vs_new, cache_k, cache_v, *ffn_weights)


def _post_sample_kernel(h_ref, att_ref, pin_ref, p_ref, spool_ref, sconv_ref,
                        gsub_ref, wpool_ref, pscale_ref, wout_ref, ln2_ref,
                        wug_ref, wuv_ref, cw_ref, cb_ref, wdn_ref, lnp_ref, wpg_ref, wpp_ref,
                        y_ref, u_ref, a2_scr, act_scr, *, sub_scale):
    pin = pin_ref[...]
    d_groups = []
    for g, w in enumerate(POOL_WINDOWS):
        gs = slice(g * POOL_GROUP_DIM, (g + 1) * POOL_GROUP_DIM)
        tot = pin[:, gs]
        for back in range(1, w):
            tot = tot + spool_ref[POOL_BUF - back, :, gs]
        d_groups.append(tot / float(w) - pin[:, gs])
    pooled = _pool_project(d_groups, wpool_ref, pscale_ref)

    h1 = _mix_residual(h_ref[...], att_ref[...], pooled, gsub_ref, wout_ref, sub_scale)

    a2_scr[...] = _rms(h1, ln2_ref[...]).astype(BF16)
    for c in range(FF_PAD // FF_CHUNK):
        cs = slice(c * FF_CHUNK, (c + 1) * FF_CHUNK)
        conv = []
        for half, w_ref in enumerate((wug_ref, wuv_ref)):
            hs = slice(half * FF_PAD + c * FF_CHUNK, half * FF_PAD + (c + 1) * FF_CHUNK)
            u = _dot(a2_scr[...], w_ref[:, cs])
            u_ref[:, hs] = u
            acc = cb_ref[:, hs] + u * cw_ref[CONV_W - 1:CONV_W, hs]
            for j in range(CONV_W - 1):
                acc = acc + sconv_ref[j, :, hs] * cw_ref[j:j + 1, hs]
            conv.append(acc)
        act_scr[:, cs] = _gelu_gate(conv[0], conv[1]).astype(BF16)
    h2 = h1 + _dot(act_scr[...], wdn_ref[...])

    y_ref[...] = _ple(h2, p_ref[...], lnp_ref, wpg_ref, wpp_ref)


def _post_sample(h, att, pin, p, spool, sconv, weights, sub_scale):
    n = h.shape[0]
    args = (h, att, pin, p, spool, sconv) + tuple(weights)
    full = lambda arr: pl.BlockSpec(arr.shape, lambda i: (0,) * arr.ndim, pipeline_mode=pl.Buffered(1))
    kernel = functools.partial(_post_sample_kernel, sub_scale=sub_scale)
    return pl.pallas_call(
        kernel,
        grid=(1,),
        in_specs=[full(a) for a in args],
        out_specs=(pl.BlockSpec((n, D_MODEL), lambda i: (0, 0)),
                   pl.BlockSpec((n, 2 * FF_PAD), lambda i: (0, 0))),
        out_shape=(jax.ShapeDtypeStruct((n, D_MODEL), F32),
                   jax.ShapeDtypeStruct((n, 2 * FF_PAD), F32)),
        scratch_shapes=[pltpu.VMEM((n, D_MODEL), BF16), pltpu.VMEM((n, FF_PAD), BF16)],
        compiler_params=pltpu.CompilerParams(
            dimension_semantics=("arbitrary",), vmem_limit_bytes=VMEM_LIMIT_BYTES),
        name="post_sample",
    )(*args)


def _split_ff(x):
    pad = [(0, 0)] * (x.ndim - 1) + [(0, FF_PAD - D_FF)]
    return jnp.concatenate([jnp.pad(x[..., :D_FF], pad), jnp.pad(x[..., D_FF:], pad)], axis=-1)


def _merge_ff(x):
    return jnp.concatenate([x[..., :D_FF], x[..., FF_PAD:FF_PAD + D_FF]], axis=-1)


def _layer(l, h_p, h_s, cache_k, cache_v, state_pool, state_conv, page_table, p_prompt, p_sample,
           ln1, w_in, g_q, g_k, lam_q1, lam_k1, lam_q2, lam_k2, g_sub, w_pool, pool_scale, w_out,
           ln2, w_up, conv_w, conv_b, w_down, ln_ple, w_pg, w_pp):
    b, s, _ = h_p.shape
    n_req = h_s.shape[0]
    n_pool = cache_k.shape[1]
    lam_init = 0.8 - 0.6 * math.exp(-0.3 * l)
    sub_scale = 1.0 - lam_init

    row = lambda x: x[l].reshape(1, -1)
    win_b = w_in[l].astype(BF16)
    gq = jnp.tile(g_q[l], QK_WIDTH // HEAD_DIM).reshape(1, -1) * (HEAD_DIM ** -0.5)
    gk = jnp.tile(g_k[l], QK_WIDTH // HEAD_DIM).reshape(1, -1)
    grp = jnp.arange(MXU_DIM) // HEAD_DIM
    gmat = jnp.where(grp[:, None] == grp[None, :], 1.0 / HEAD_DIM, 0.0).astype(BF16)
    lamp = jnp.stack([lam_q1[l], lam_k1[l], lam_q2[l], lam_k2[l]])
    wup_b = w_up[l].astype(BF16)
    ffpad = ((0, 0), (0, FF_PAD - D_FF))
    tail_weights = (
        row(g_sub), w_pool[l].astype(BF16), row(pool_scale), w_out[l].astype(BF16), row(ln2),
        jnp.pad(wup_b[:, :D_FF], ffpad), jnp.pad(wup_b[:, D_FF:], ffpad),
        _split_ff(conv_w[l]), _split_ff(conv_b[l].reshape(1, -1)),
        jnp.pad(w_down[l].astype(BF16), ((0, FF_PAD - D_FF), (0, 0))),
        row(ln_ple), w_pg[l].astype(BF16), w_pp[l].astype(BF16),
    )

    qs, kts, v4s, pins, kbs, vbs = _project(h_s.reshape(n_req, D_MODEL), row(ln1), win_b, gmat, gq, gk,
                                            tm=n_req, seq=n_req)
    ck_view = jnp.transpose(cache_k[l], (0, 2, 3, 4, 1)).reshape(n_pool, QK_WIDTH, PAGE_SIZE)
    cv_view = cache_v[l].reshape(n_pool, PAGE_ROWS, V_DIM)
    stream_args = (qs.astype(F32), kbs.astype(F32), vbs.astype(F32), ck_view, cv_view)

    shp = lambda x: x.reshape(b, s, x.shape[-1])
    n_proj_req = n_req * HOST_SHARE[0] // sum(HOST_SHARE)
    n_attn_req = n_req * HOST_SHARE[1] // sum(HOST_SHARE)
    n_post_req = n_req - n_proj_req - n_attn_req
    q, kt, v4, pin, kb, vb, att_s0 = _project(
        h_p.reshape(b * s, D_MODEL), row(ln1), win_b, gmat, gq, gk, tm=PROJ_TILE, seq=s,
        host=(page_table, lamp, *stream_args, lam_init, 0, n_proj_req, STREAM_PAGES))
    n_mix = 4
    h1, att_s1 = _prompt_attention(page_table, lamp, shp(q), shp(kb), shp(vb), h_p, shp(pin), *stream_args,
                                   tail_weights[:n_mix], lam_init, sub_scale, blk=TOKEN_TILE,
                                   first_req=n_proj_req, n_host_req=n_attn_req, ppc=STREAM_PAGES)
    y_p, ctail, att_s2 = _post_prompt(page_table, lamp, h1, p_prompt[l], *stream_args,
                                      tail_weights[n_mix:], lam_init, tm=TOKEN_TILE,
                                      first_req=n_proj_req + n_attn_req, n_host_req=n_post_req,
                                      ppc=STREAM_PAGES)
    untranspose_k = lambda x, n: jnp.transpose(x.reshape(-1, N_HEADS, 2, HEAD_DIM, n), (0, 4, 1, 2, 3))
    k_p = untranspose_k(kt, s)
    v_p = v4.reshape(b, s, N_HEADS, V_DIM)
    pool_p = shp(pin)[:, s - POOL_BUF:]
    conv_p = _merge_ff(ctail[:, CONV_HALO - (CONV_W - 1):])

    att_s = jnp.concatenate([att_s0[:, 0], att_s1[:, 0], att_s2[:, 0]], axis=0)
    sconv = jnp.moveaxis(_split_ff(state_conv[l]), 1, 0)
    y_s, u_s = _post_sample(h_s.reshape(n_req, D_MODEL), att_s, pins, p_sample[l].reshape(n_req, PLE_DIM),
                            jnp.moveaxis(state_pool[l], 1, 0), sconv,
                            tail_weights, sub_scale)
    k_s = untranspose_k(kts, n_req).reshape(n_req, 1, N_HEADS, 2, HEAD_DIM)
    v_s = v4s.reshape(n_req, 1, N_HEADS, V_DIM)
    pool_s = jnp.concatenate([state_pool[l][:, 1:], pins[:, None]], axis=1)
    conv_s = jnp.concatenate([state_conv[l][:, 1:], _merge_ff(u_s)[:, None]], axis=1)

    return (y_p, y_s.reshape(n_req, 1, D_MODEL)), (k_p, v_p, pool_p, conv_p, k_s, v_s, pool_s, conv_s)


def kernel(x_prompt, x_sample, cache_k, cache_v, state_pool, state_conv, page_table, p_prompt, p_sample, ln1, w_in, g_q, g_k, lam_q1, lam_k1, lam_q2, lam_k2, g_sub, w_pool, pool_scale, w_out, ln2, w_up, conv_w, conv_b, w_down, ln_ple, w_pg, w_pp):
    depth = ln1.shape[0]
    h_p, h_s = x_prompt, x_sample
    per_layer = []
    for l in range(depth):
        (h_p, h_s), outs = _layer(
            l, h_p, h_s, cache_k, cache_v, state_pool, state_conv, page_table, p_prompt, p_sample,
            ln1, w_in, g_q, g_k, lam_q1, lam_k1, lam_q2, lam_k2, g_sub, w_pool, pool_scale, w_out,
            ln2, w_up, conv_w, conv_b, w_down, ln_ple, w_pg, w_pp)
        per_layer.append(outs)
    stacked = tuple(jnp.stack(xs) for xs in zip(*per_layer))
    return (h_p, h_s) + stacked
```

```python
import functools
import math

import jax
import jax.numpy as jnp
from jax import lax
from jax.experimental import pallas as pl
from jax.experimental.pallas import tpu as pltpu

F32 = jnp.float32
BF16 = jnp.bfloat16

D_MODEL = 1024
N_HEADS = 4
HEAD_DIM = 64
V_DIM = 2 * HEAD_DIM
QK_WIDTH = N_HEADS * 2 * HEAD_DIM
ATTN_WIDTH = N_HEADS * V_DIM
POOL_WINDOWS = (2, 4, 8, 16)
POOL_WIDTH = D_MODEL - ATTN_WIDTH
POOL_GROUP_DIM = POOL_WIDTH // len(POOL_WINDOWS)
POOL_BUF = max(POOL_WINDOWS) - 1
IN_WIDTH = 2 * QK_WIDTH + ATTN_WIDTH + POOL_WIDTH
D_FF = 2752
CONV_W = 3
PLE_DIM = 256
PAGE_SIZE = 128
PAGE_ROWS = PAGE_SIZE * N_HEADS
EPS = 1e-6
NEG = -1e30

LANES = 128
SUBLANES = 8
MXU_DIM = 256
N_DMA_PRIORITIES = 2
STREAM_AHEAD = 3
STREAM_SLOTS = STREAM_AHEAD + 2
VMEM_LIMIT_BYTES = 60 * 1024 * 1024

FF_PAD = -(-D_FF // MXU_DIM) * MXU_DIM
FF_CHUNK = MXU_DIM
TOKEN_TILE = 256
PROJ_TILE = 512
HOST_SHARE = (4, 7, 5)
STREAM_PAGES = 8
POOL_HALO = 16
CONV_HALO = SUBLANES


def _rms(x, g):
    ms = jnp.mean(x * x, axis=-1, keepdims=True)
    return x * lax.rsqrt(ms + EPS) * g


def _dot(a, b):
    return jnp.dot(a, b, preferred_element_type=F32)


def _dot_nt(a, b):
    return lax.dot_general(a, b, (((1,), (1,)), ((), ())), preferred_element_type=F32)


def _lambda(lamp_ref, lam_init):
    p = lamp_ref[...]
    s1 = jnp.sum(p[0:1] * p[1:2], axis=-1, keepdims=True)
    s2 = jnp.sum(p[2:3] * p[3:4], axis=-1, keepdims=True)
    return jnp.exp(s1) - jnp.exp(s2) + lam_init


def _proj_kernel(*refs, stream_cfg, lam_init):
    if stream_cfg is None:
        (h_ref, ln1_ref, win_ref, gmat_ref, gq_ref, gk_ref,
         q_ref, k_ref, v_ref, pin_ref, kb_ref, vb_ref, a_scr) = refs
        stream = None
    else:
        (pt_ref, h_ref, ln1_ref, win_ref, gmat_ref, gq_ref, gk_ref, lamp_ref,
         qs_ref, ks_ref, vs_ref, ck_hbm, cv_hbm,
         q_ref, k_ref, v_ref, pin_ref, kb_ref, vb_ref, os_ref, a_scr, *stream_scr) = refs
        step = pl.program_id(0)
        stream = _DecodeStream(pt_ref, (qs_ref, ks_ref, vs_ref), (ck_hbm, cv_hbm), os_ref, stream_scr,
                               _lambda(lamp_ref, lam_init), **stream_cfg)

        @pl.when(step == 0)
        def _():
            stream.prime()

        g0 = step * stream.chunks_per_step
        stream.fetch(g0)
        s_dec = stream.scores(g0)

    tm = h_ref.shape[0]
    a_scr[...] = _rms(h_ref[...], ln1_ref[...]).astype(BF16)
    gmat = gmat_ref[...]
    n_chunks = IN_WIDTH // MXU_DIM
    per_seg = QK_WIDTH // MXU_DIM
    for c in range(n_chunks):
        hosted = stream is not None and c < stream.chunks_per_step
        prefetch = stream is not None and c + 1 < stream.chunks_per_step
        if prefetch:
            stream.fetch(g0 + c + 1)
        z = _dot(a_scr[...], win_ref[:, c * MXU_DIM:(c + 1) * MXU_DIM])
        seg, off = divmod(c, per_seg)
        sl = slice(off * MXU_DIM, (off + 1) * MXU_DIM)
        if seg < 2:
            ms = _dot((z * z).astype(BF16), gmat)
            y = z * lax.rsqrt(ms + EPS)
            if seg == 0:
                q_ref[:, sl] = (y * gq_ref[:, sl]).astype(BF16)
            else:
                y = y * gk_ref[:, sl]
                k_ref[0, sl, :] = y.T
                kb_ref[:, sl] = y.astype(BF16)
        elif seg == 2:
            for hh in range(MXU_DIM // V_DIM):
                head = off * (MXU_DIM // V_DIM) + hh
                v_ref[pl.ds(head, tm, stride=N_HEADS), :] = z[:, hh * V_DIM:(hh + 1) * V_DIM]
            vb_ref[:, sl] = z.astype(BF16)
        else:
            pin_ref[:, sl] = z
        if prefetch:
            s_next = stream.scores(g0 + c + 1)
        if hosted:
            stream.update(g0 + c, s_dec)
            s_dec = s_next if prefetch else None


def _project(h2d, ln1, win_b, gmat, gq, gk, tm, seq, host=None):
    t = h2d.shape[0]
    assert t % seq == 0 and seq % tm == 0
    tiles = seq // tm
    n_steps = t // tm
    row = lambda i, *_: (i, 0)
    const = lambda i, *_: (0, 0)
    w512 = pl.BlockSpec((tm, QK_WIDTH), row)
    out_shape = [
        jax.ShapeDtypeStruct((t, QK_WIDTH), BF16),
        jax.ShapeDtypeStruct((t // seq, QK_WIDTH, seq), F32),
        jax.ShapeDtypeStruct((t * N_HEADS, V_DIM), F32),
        jax.ShapeDtypeStruct((t, POOL_WIDTH), F32),
        jax.ShapeDtypeStruct((t, QK_WIDTH), BF16),
        jax.ShapeDtypeStruct((t, ATTN_WIDTH), BF16),
    ]
    out_specs = [
        w512,
        pl.BlockSpec((1, QK_WIDTH, tm), lambda i, *_: (i // tiles, 0, i % tiles)),
        pl.BlockSpec((tm * N_HEADS, V_DIM), row),
        w512, w512, w512,
    ]
    in_specs = [
        pl.BlockSpec((tm, D_MODEL), row),
        pl.BlockSpec((1, D_MODEL), const),
        pl.BlockSpec((D_MODEL, IN_WIDTH), const, pipeline_mode=pl.Buffered(1)),
        pl.BlockSpec((MXU_DIM, MXU_DIM), const),
        pl.BlockSpec((1, QK_WIDTH), const),
        pl.BlockSpec((1, QK_WIDTH), const),
    ]
    scratch = [pltpu.VMEM((tm, D_MODEL), BF16)]
    args = (h2d, ln1, win_b, gmat, gq, gk)
    params = pltpu.CompilerParams(dimension_semantics=("arbitrary",), vmem_limit_bytes=VMEM_LIMIT_BYTES)
    if host is None:
        kernel = functools.partial(_proj_kernel, stream_cfg=None, lam_init=None)
        return pl.pallas_call(
            kernel, grid=(n_steps,), in_specs=in_specs, out_specs=out_specs, out_shape=out_shape,
            scratch_shapes=scratch, compiler_params=params, name="proj",
        )(*args)

    page_table, lamp, qs, ks_new, vs_new, cache_k, cache_v, lam_init, first_req, n_host_req, ppc = host
    stream_cfg = _stream_cfg(first_req, n_host_req, n_steps, page_table.shape[1], ppc)
    assert stream_cfg["chunks_per_step"] <= IN_WIDTH // MXU_DIM
    os_spec, os_shape = _stream_out(n_host_req)
    grid_spec = pltpu.PrefetchScalarGridSpec(
        num_scalar_prefetch=1,
        grid=(n_steps,),
        in_specs=in_specs + [pl.BlockSpec((4, HEAD_DIM), const)] + _stream_specs(qs.shape[0]),
        out_specs=out_specs + [os_spec],
        scratch_shapes=scratch + _stream_scratch(ppc),
    )
    kernel = functools.partial(_proj_kernel, stream_cfg=stream_cfg, lam_init=lam_init)
    return pl.pallas_call(
        kernel, grid_spec=grid_spec, out_shape=out_shape + [os_shape], compiler_params=params,
        name="proj_host",
    )(page_table.reshape(-1), *args, lamp, qs, ks_new, vs_new, cache_k, cache_v)


def _stream_scratch(ppc):
    return [
        pltpu.VMEM((STREAM_SLOTS, ppc, QK_WIDTH, PAGE_SIZE), F32),
        pltpu.VMEM((STREAM_SLOTS, ppc * PAGE_ROWS, V_DIM), F32),
        pltpu.SemaphoreType.DMA((STREAM_SLOTS, 2)),
        pltpu.VMEM((2, 2 * N_HEADS, QK_WIDTH), BF16),
        pltpu.VMEM((2 * N_HEADS, LANES), F32),
        pltpu.VMEM((2 * N_HEADS, LANES), F32),
        pltpu.VMEM((2 * N_HEADS, ATTN_WIDTH), F32),
    ]


class _DecodeStream:
    def __init__(self, pt_ref, rows, caches, out_ref, scratch, lam, *,
                 first_req, n_steps, n_pages, ppc, chunks_per_step):
        self.pt_ref, self.out_ref, self.lam = pt_ref, out_ref, lam
        self.q_ref, self.k_ref, self.v_ref = rows
        self.ck_hbm, self.cv_hbm = caches
        (self.kbuf, self.vbuf, self.sem, self.qbd, self.m, self.l, self.acc) = scratch
        self.first_req, self.n_pages, self.ppc = first_req, n_pages, ppc
        self.cpr = n_pages // ppc
        self.chunks_per_step = chunks_per_step
        self.total = n_steps * chunks_per_step
        assert self.total % self.cpr == 0
        self.hc = lax.broadcasted_iota(jnp.int32, (2 * N_HEADS, QK_WIDTH), 0)

    def _request(self, g):
        return self.first_req + lax.div(g, self.cpr)

    def _copies(self, g, slot, real):
        base = self._request(g) * self.n_pages + lax.rem(g, self.cpr) * self.ppc
        copies = []
        for p in range(self.ppc):
            page = self.pt_ref[base + p] if real else 0
            rows = pl.ds(p * PAGE_ROWS, PAGE_ROWS)
            copies.append(pltpu.make_async_copy(self.ck_hbm.at[page], self.kbuf.at[slot, p],
                                                self.sem.at[slot, 0]))
            copies.append(pltpu.make_async_copy(self.cv_hbm.at[page], self.vbuf.at[slot, rows],
                                                self.sem.at[slot, 1]))
        return copies

    def _start(self, g):
        for n, cp in enumerate(self._copies(g, lax.rem(g, STREAM_SLOTS), True)):
            cp.start(priority=n % N_DMA_PRIORITIES)

    def prime(self):
        for g in range(STREAM_AHEAD):
            self._start(jnp.int32(g))
        for ref in (self.m, self.l, self.acc):
            ref[...] = jnp.zeros(ref.shape, F32)

    def fetch(self, g):
        ahead = g + STREAM_AHEAD

        @pl.when(ahead < self.total)
        def _():
            self._start(ahead)

        for cp in self._copies(g, lax.rem(g, STREAM_SLOTS), False):
            cp.wait()

    def scores(self, g):
        req = self._request(g)
        parity = lax.rem(req, 2)
        grp = lax.broadcasted_iota(jnp.int32, self.hc.shape, 1) // HEAD_DIM
        q_row = self.q_ref[pl.ds(req, 1), :]
        qbd = jnp.where(self.hc == grp, jnp.broadcast_to(q_row, self.hc.shape), 0.0).astype(BF16)
        self.qbd[parity] = qbd
        slot = lax.rem(g, STREAM_SLOTS)
        return jnp.concatenate(
            [_dot(qbd, self.kbuf[slot, p].astype(BF16)) for p in range(self.ppc)], axis=1)

    def update(self, g, s):
        req = self._request(g)
        first = lax.rem(g, self.cpr) == 0
        qbd = self.qbd[lax.rem(req, 2)].astype(F32)
        s_self = jnp.sum(qbd * self.k_ref[pl.ds(req, 1), :], axis=-1, keepdims=True)
        m_prev = jnp.where(first, jnp.broadcast_to(s_self, self.m.shape), self.m[...])
        l_prev = jnp.where(first, 1.0, self.l[...])
        acc_prev = jnp.where(first, jnp.broadcast_to(self.v_ref[pl.ds(req, 1), :], self.acc.shape), self.acc[...])

        slot = lax.rem(g, STREAM_SLOTS)
        n_tok = self.ppc * PAGE_SIZE
        m_new = jnp.maximum(m_prev, jnp.max(s, axis=1, keepdims=True))
        alpha = jnp.exp(m_prev - m_new)
        p = jnp.exp(s - m_new[:, :1])
        l_new = alpha * l_prev + jnp.sum(p, axis=1, keepdims=True)
        pb = p.astype(BF16)
        pv = [_dot(pb, self.vbuf[slot, pl.ds(h, n_tok, stride=N_HEADS), :].astype(BF16))
              for h in range(N_HEADS)]
        acc_new = alpha[:, :1] * acc_prev + jnp.concatenate(pv, axis=1)
        self.l[...] = l_new
        self.acc[...] = acc_new
        self.m[...] = m_new

        coef = jnp.where(lax.rem(self.hc, 2) == 0, 1.0, -self.lam)
        head_of_lane = lax.broadcasted_iota(jnp.int32, self.hc.shape, 1) // V_DIM
        keep = head_of_lane == self.hc // 2
        o = jnp.where(keep, coef * acc_new / l_new[:, :1], 0.0)
        row = jnp.sum(o, axis=0, keepdims=True)
        self.out_ref[req - self.first_req] = jnp.broadcast_to(row, self.out_ref.shape[1:])


def _stream_specs(n_req):
    const = lambda *ids: (0, 0)
    rows = pl.BlockSpec((n_req, QK_WIDTH), const)
    return [rows, rows, rows, pl.BlockSpec(memory_space=pl.ANY), pl.BlockSpec(memory_space=pl.ANY)]


def _stream_out(n_host_req):
    return (pl.BlockSpec((n_host_req, SUBLANES, ATTN_WIDTH), lambda *ids: (0, 0, 0)),
            jax.ShapeDtypeStruct((n_host_req, SUBLANES, ATTN_WIDTH), F32))


def _stream_cfg(first_req, n_host_req, n_steps, n_pages, ppc):
    chunks = n_host_req * (n_pages // ppc)
    assert n_pages % ppc == 0 and chunks % n_steps == 0, "the hosted requests must fill the grid evenly"
    return dict(first_req=first_req, n_steps=n_steps, n_pages=n_pages, ppc=ppc,
                chunks_per_step=chunks // n_steps)


def _mix_residual(h, att, pooled, gsub_ref, wout_ref, sub_scale):
    parts = [_rms(att[:, hd * V_DIM:(hd + 1) * V_DIM], gsub_ref[...]) * sub_scale for hd in range(N_HEADS)]
    mixed = jnp.concatenate(parts + [pooled], axis=1).astype(BF16)
    return h + _dot(mixed, wout_ref[...])


def _pool_project(d_groups, wpool_ref, pscale_ref):
    ys = [_dot(d.astype(BF16), wpool_ref[g]) for g, d in enumerate(d_groups)]
    return jnp.concatenate(ys, axis=1) * pscale_ref[...]


def _gelu_gate(g, val):
    return 0.5 * g * (1.0 + lax.erf(g * (1.0 / math.sqrt(2.0)))) * val


def _ple(h, p, lnp_ref, wpg_ref, wpp_ref):
    gate = jax.nn.sigmoid(_dot(_rms(h, lnp_ref[...]).astype(BF16), wpg_ref[...]))
    return h + gate * _dot(p.astype(BF16), wpp_ref[...])


def _pool_mix(pin, tile_idx, pbuf, h, att, gsub_ref, wpool_ref, pscale_ref, wout_ref, sub_scale):
    tm = pin.shape[0]

    @pl.when(tile_idx == 0)
    def _():
        pbuf[0:POOL_HALO, :] = jnp.zeros((POOL_HALO, POOL_WIDTH), F32)

    pbuf[POOL_HALO:, :] = pin
    pos = tile_idx * tm + lax.broadcasted_iota(jnp.int32, (tm, 1), 0)
    d_groups = []
    for g, w in enumerate(POOL_WINDOWS):
        gs = slice(g * POOL_GROUP_DIM, (g + 1) * POOL_GROUP_DIM)
        tot = pin[:, gs]
        for back in range(1, w):
            tot = tot + pbuf[POOL_HALO - back:POOL_HALO - back + tm, gs]
        cnt = jnp.minimum(w, pos + 1).astype(F32)
        d_groups.append(tot / cnt - pin[:, gs])
    pbuf[0:POOL_HALO, :] = pbuf[tm:tm + POOL_HALO, :]
    pooled = _pool_project(d_groups, wpool_ref, pscale_ref)
    return _mix_residual(h, att, pooled, gsub_ref, wout_ref, sub_scale)


def _attn_kernel(pt_ref, lamp_ref, q_ref, k_ref, v_ref, h_ref, pin_ref, qs_ref, ks_ref, vs_ref, ck_hbm, cv_hbm,
                 gsub_ref, wpool_ref, pscale_ref, wout_ref,
                 h1_ref, os_ref, qq_scr, m_scr, l_scr, acc_scr, pbuf, *stream_scr,
                 blk, nq, lam_init, sub_scale, stream_cfg):
    i = pl.program_id(1)
    t = pl.program_id(0) * nq + i
    lam = _lambda(lamp_ref, lam_init)
    stream = _DecodeStream(pt_ref, (qs_ref, ks_ref, vs_ref), (ck_hbm, cv_hbm), os_ref, stream_scr, lam,
                           **stream_cfg)

    @pl.when(t == 0)
    def _():
        stream.prime()

    g0 = t * stream.chunks_per_step

    lane = lax.broadcasted_iota(jnp.int32, (blk, V_DIM), 1)
    row = lax.broadcasted_iota(jnp.int32, (2 * blk, blk), 0)
    col = lax.broadcasted_iota(jnp.int32, (2 * blk, blk), 1)
    causal = col <= jnp.where(row >= blk, row - blk, row)
    reps = blk // LANES
    heads = [slice(h * V_DIM, (h + 1) * V_DIM) for h in range(N_HEADS)]

    for h, hs in enumerate(heads):
        qh = q_ref[0, :, hs]
        zero = jnp.zeros_like(qh)
        qq_scr[h, 0:blk, :] = jnp.where(lane < HEAD_DIM, qh, zero)
        qq_scr[h, blk:2 * blk, :] = jnp.where(lane >= HEAD_DIM, qh, zero)
    m_scr[...] = jnp.full(m_scr.shape, -jnp.inf, F32)
    l_scr[...] = jnp.zeros(l_scr.shape, F32)
    acc_scr[...] = jnp.zeros(acc_scr.shape, F32)

    def attn_scores(j, masked, which):
        start = pl.multiple_of(j * blk, blk)
        out = []
        for h in which:
            s = _dot_nt(qq_scr[h], k_ref[0, pl.ds(start, blk), heads[h]])
            out.append(jnp.where(causal, s, NEG) if masked else s)
        return out

    def attn_update(j, scores, which):
        start = pl.multiple_of(j * blk, blk)
        for s, h in zip(scores, which):
            hs = heads[h]
            m_prev = m_scr[h]
            m_new = jnp.maximum(m_prev, jnp.max(s, axis=1, keepdims=True))
            alpha = jnp.exp(m_prev - m_new)
            p = jnp.exp(s - jnp.concatenate([m_new] * reps, axis=1))
            l_scr[h] = alpha * l_scr[h] + jnp.sum(p, axis=1, keepdims=True)
            acc_scr[h] = alpha * acc_scr[h] + _dot(p.astype(BF16), v_ref[0, pl.ds(start, blk), hs])
            m_scr[h] = m_new

    stream.fetch(g0)
    s0 = stream.scores(g0)

    def prompt_block(j, between=lambda: None, after=lambda: None):
        half = N_HEADS // 2
        first = attn_scores(j, False, range(half))
        between()
        attn_update(j, first, range(half))
        second = attn_scores(j, False, range(half, N_HEADS))
        after()
        attn_update(j, second, range(half, N_HEADS))

    def body(j, s):
        stream.fetch(g0 + j + 1)
        nxt = []
        prompt_block(j, lambda: nxt.append(stream.scores(g0 + j + 1)), lambda: stream.update(g0 + j, s))
        return nxt[0]

    def plain(j, carry):
        prompt_block(j)
        return carry

    def rest(c, s):
        stream.fetch(g0 + c + 1)
        s_next = stream.scores(g0 + c + 1)
        stream.update(g0 + c, s)
        return s_next

    last = stream.chunks_per_step - 1
    n_hosted = jnp.minimum(i, last)
    s_dec = lax.fori_loop(0, n_hosted, body, s0)
    lax.fori_loop(n_hosted, i, plain, 0)
    attn_update(i, attn_scores(i, True, range(N_HEADS)), range(N_HEADS))
    s_dec = lax.fori_loop(n_hosted, last, rest, s_dec)
    stream.update(g0 + last, s_dec)
    att = []
    for h in range(N_HEADS):
        o = acc_scr[h] / l_scr[h]
        att.append(o[:blk] - lam * o[blk:])
    h1_ref[0] = _pool_mix(pin_ref[0], i, pbuf, h_ref[0], jnp.concatenate(att, axis=1),
                          gsub_ref, wpool_ref, pscale_ref, wout_ref, sub_scale)


def _prompt_attention(page_table, lamp, q, kb, vb, h, pin, qs, ks_new, vs_new, cache_k, cache_v,
                      mix_weights, lam_init, sub_scale, blk, first_req, n_host_req, ppc):
    b, s, _ = q.shape
    assert s % blk == 0
    nq = s // blk
    stream_cfg = _stream_cfg(first_req, n_host_req, b * nq, page_table.shape[1], ppc)
    kernel = functools.partial(_attn_kernel, blk=blk, nq=nq, lam_init=lam_init, sub_scale=sub_scale,
                               stream_cfg=stream_cfg)
    os_spec, os_shape = _stream_out(n_host_req)
    tile = lambda width: pl.BlockSpec((1, blk, width), lambda bi, i, pt: (bi, i, 0))
    resident = lambda arr: pl.BlockSpec(arr.shape, lambda bi, i, pt: (0,) * arr.ndim,
                                        pipeline_mode=pl.Buffered(1))
    grid_spec = pltpu.PrefetchScalarGridSpec(
        num_scalar_prefetch=1,
        grid=(b, nq),
        in_specs=[
            pl.BlockSpec((4, HEAD_DIM), lambda bi, i, pt: (0, 0)),
            tile(QK_WIDTH),
            pl.BlockSpec((1, s, QK_WIDTH), lambda bi, i, pt: (bi, 0, 0)),
            pl.BlockSpec((1, s, ATTN_WIDTH), lambda bi, i, pt: (bi, 0, 0)),
            tile(D_MODEL), tile(POOL_WIDTH),
        ] + _stream_specs(qs.shape[0]) + [resident(w) for w in mix_weights],
        out_specs=(tile(D_MODEL), os_spec),
        scratch_shapes=[pltpu.VMEM((N_HEADS, 2 * blk, V_DIM), BF16)]
        + [pltpu.VMEM((N_HEADS, 2 * blk, V_DIM), F32)] * 3
        + [pltpu.VMEM((blk + POOL_HALO, POOL_WIDTH), F32)] + _stream_scratch(ppc),
    )
    return pl.pallas_call(
        kernel,
        grid_spec=grid_spec,
        out_shape=(jax.ShapeDtypeStruct((b, s, D_MODEL), F32), os_shape),
        compiler_params=pltpu.CompilerParams(
            dimension_semantics=("arbitrary", "arbitrary"), vmem_limit_bytes=VMEM_LIMIT_BYTES),
        name="prompt_attn",
    )(page_table.reshape(-1), lamp, q, kb, vb, h, pin, qs, ks_new, vs_new, cache_k, cache_v, *mix_weights)


def _post_prompt_kernel(pt_ref, lamp_ref, h1_ref, p_ref,
                        qs_ref, ks_ref, vs_ref, ck_hbm, cv_hbm,
                        ln2_ref, wug_ref, wuv_ref, cw_ref, cb_ref, wdn_ref, lnp_ref, wpg_ref, wpp_ref,
                        y_ref, ctail_ref, os_ref,
                        ubuf_g, ubuf_v, ucarry, a2_scr, act_scr, *stream_scr,
                        tm, n_tiles, lam_init, stream_cfg):
    t = pl.program_id(1)
    step = pl.program_id(0) * n_tiles + t
    stream = _DecodeStream(pt_ref, (qs_ref, ks_ref, vs_ref), (ck_hbm, cv_hbm), os_ref, stream_scr,
                           _lambda(lamp_ref, lam_init), **stream_cfg)

    @pl.when(step == 0)
    def _():
        stream.prime()

    g0 = step * stream.chunks_per_step
    stream.fetch(g0)
    s_dec = stream.scores(g0)

    @pl.when(t == 0)
    def _():
        ucarry[...] = jnp.zeros(ucarry.shape, F32)

    h1 = h1_ref[0]

    a2_scr[...] = _rms(h1, ln2_ref[...]).astype(BF16)
    halves = ((wug_ref, ubuf_g), (wuv_ref, ubuf_v))

    def columns(half, c):
        return slice(half * FF_PAD + c * FF_CHUNK, half * FF_PAD + (c + 1) * FF_CHUNK)

    def ffn_up(c):
        for half, (w_ref, ubufs) in enumerate(halves):
            ubuf = ubufs.at[c % 2]
            ubuf[0:CONV_HALO, :] = ucarry[:, columns(half, c)]
            ubuf[CONV_HALO:, :] = _dot(a2_scr[...], w_ref[:, c * FF_CHUNK:(c + 1) * FF_CHUNK])

    def ffn_activate(c):
        conv = []
        for half, (_, ubufs) in enumerate(halves):
            ubuf = ubufs.at[c % 2]
            hs = columns(half, c)
            acc = cb_ref[:, hs]
            for j in range(CONV_W):
                lo = CONV_HALO - (CONV_W - 1) + j
                acc = acc + ubuf[lo:lo + tm, :] * cw_ref[j:j + 1, hs]
            ucarry[:, hs] = ubuf[tm:tm + CONV_HALO, :]
            conv.append(acc)
        act_scr[:, c * FF_CHUNK:(c + 1) * FF_CHUNK] = _gelu_gate(conv[0], conv[1]).astype(BF16)

    n_ffn = FF_PAD // FF_CHUNK
    ffn_up(0)
    for c in range(n_ffn):
        nxt = c + 1
        if nxt < stream.chunks_per_step:
            stream.fetch(g0 + nxt)
        if nxt < n_ffn:
            ffn_up(nxt)
        if nxt < stream.chunks_per_step:
            s_next = stream.scores(g0 + nxt)
        ffn_activate(c)
        if c < stream.chunks_per_step:
            stream.update(g0 + c, s_dec)
            s_dec = s_next
    ctail_ref[0] = ucarry[...]
    h2 = h1 + _dot(act_scr[...], wdn_ref[...])

    y_ref[0] = _ple(h2, p_ref[0], lnp_ref, wpg_ref, wpp_ref)


def _post_prompt(page_table, lamp, h1, p, qs, ks_new, vs_new, cache_k, cache_v, ffn_weights,
                 lam_init, tm, first_req, n_host_req, ppc):
    b, s, _ = h1.shape
    assert s % tm == 0
    n_tiles = s // tm
    tile = lambda width: pl.BlockSpec((1, tm, width), lambda bi, t, pt: (bi, t, 0))
    resident = lambda arr: pl.BlockSpec(arr.shape, lambda bi, t, pt: (0,) * arr.ndim,
                                        pipeline_mode=pl.Buffered(1))
    stream_cfg = _stream_cfg(first_req, n_host_req, b * n_tiles, page_table.shape[1], ppc)
    assert stream_cfg["chunks_per_step"] <= FF_PAD // FF_CHUNK
    kernel = functools.partial(_post_prompt_kernel, tm=tm, n_tiles=n_tiles, lam_init=lam_init,
                               stream_cfg=stream_cfg)
    os_spec, os_shape = _stream_out(n_host_req)
    grid_spec = pltpu.PrefetchScalarGridSpec(
        num_scalar_prefetch=1,
        grid=(b, n_tiles),
        in_specs=[pl.BlockSpec((4, HEAD_DIM), lambda bi, t, pt: (0, 0)), tile(D_MODEL), tile(PLE_DIM)]
        + _stream_specs(qs.shape[0]) + [resident(w) for w in ffn_weights],
        out_specs=(tile(D_MODEL),
                   pl.BlockSpec((1, CONV_HALO, 2 * FF_PAD), lambda bi, t, pt: (bi, 0, 0)),
                   os_spec),
        scratch_shapes=[
            pltpu.VMEM((2, tm + CONV_HALO, FF_CHUNK), F32),
            pltpu.VMEM((2, tm + CONV_HALO, FF_CHUNK), F32),
            pltpu.VMEM((CONV_HALO, 2 * FF_PAD), F32),
            pltpu.VMEM((tm, D_MODEL), BF16),
            pltpu.VMEM((tm, FF_PAD), BF16),
        ] + _stream_scratch(ppc),
    )
    return pl.pallas_call(
        kernel,
        grid_spec=grid_spec,
        out_shape=(jax.ShapeDtypeStruct((b, s, D_MODEL), F32),
                   jax.ShapeDtypeStruct((b, CONV_HALO, 2 * FF_PAD), F32),
                   os_shape),
        compiler_params=pltpu.CompilerParams(
            dimension_semantics=("arbitrary", "arbitrary"), vmem_limit_bytes=VMEM_LIMIT_BYTES),
        name="post_prompt",
    )(page_table.reshape(-1), lamp, h1, p, qs, ks_new, vs_new, cache_k, cache_v, *ffn_weights)


def _post_sample_kernel(h_ref, att_ref, pin_ref, p_ref, spool_ref, sconv_ref,
                        gsub_ref, wpool_ref, pscale_ref, wout_ref, ln2_ref,
                        wug_ref, wuv_ref, cw_ref, cb_ref, wdn_ref, lnp_ref, wpg_ref, wpp_ref,
                        y_ref, u_ref, a2_scr, act_scr, *, sub_scale):
    pin = pin_ref[...]
    d_groups = []
    for g, w in enumerate(POOL_WINDOWS):
        gs = slice(g * POOL_GROUP_DIM, (g + 1) * POOL_GROUP_DIM)
        tot = pin[:, gs]
        for back in range(1, w):
            tot = tot + spool_ref[POOL_BUF - back, :, gs]
        d_groups.append(tot / float(w) - pin[:, gs])
    pooled = _pool_project(d_groups, wpool_ref, pscale_ref)

    h1 = _mix_residual(h_ref[...], att_ref[...], pooled, gsub_ref, wout_ref, sub_scale)

    a2_scr[...] = _rms(h1, ln2_ref[...]).astype(BF16)
    for c in range(FF_PAD // FF_CHUNK):
        cs = slice(c * FF_CHUNK, (c + 1) * FF_CHUNK)
        conv = []
        for half, w_ref in enumerate((wug_ref, wuv_ref)):
            hs = slice(half * FF_PAD + c * FF_CHUNK, half * FF_PAD + (c + 1) * FF_CHUNK)
            u = _dot(a2_scr[...], w_ref[:, cs])
            u_ref[:, hs] = u
            acc = cb_ref[:, hs] + u * cw_ref[CONV_W - 1:CONV_W, hs]
            for j in range(CONV_W - 1):
                acc = acc + sconv_ref[j, :, hs] * cw_ref[j:j + 1, hs]
            conv.append(acc)
        act_scr[:, cs] = _gelu_gate(conv[0], conv[1]).astype(BF16)
    h2 = h1 + _dot(act_scr[...], wdn_ref[...])

    y_ref[...] = _ple(h2, p_ref[...], lnp_ref, wpg_ref, wpp_ref)


def _post_sample(h, att, pin, p, spool, sconv, weights, sub_scale):
    n = h.shape[0]
    args = (h, att, pin, p, spool, sconv) + tuple(weights)
    full = lambda arr: pl.BlockSpec(arr.shape, lambda i: (0,) * arr.ndim, pipeline_mode=pl.Buffered(1))
    kernel = functools.partial(_post_sample_kernel, sub_scale=sub_scale)
    return pl.pallas_call(
        kernel,
        grid=(1,),
        in_specs=[full(a) for a in args],
        out_specs=(pl.BlockSpec((n, D_MODEL), lambda i: (0, 0)),
                   pl.BlockSpec((n, 2 * FF_PAD), lambda i: (0, 0))),
        out_shape=(jax.ShapeDtypeStruct((n, D_MODEL), F32),
                   jax.ShapeDtypeStruct((n, 2 * FF_PAD), F32)),
        scratch_shapes=[pltpu.VMEM((n, D_MODEL), BF16), pltpu.VMEM((n, FF_PAD), BF16)],
        compiler_params=pltpu.CompilerParams(
            dimension_semantics=("arbitrary",), vmem_limit_bytes=VMEM_LIMIT_BYTES),
        name="post_sample",
    )(*args)


def _split_ff(x):
    pad = [(0, 0)] * (x.ndim - 1) + [(0, FF_PAD - D_FF)]
    return jnp.concatenate([jnp.pad(x[..., :D_FF], pad), jnp.pad(x[..., D_FF:], pad)], axis=-1)


def _merge_ff(x):
    return jnp.concatenate([x[..., :D_FF], x[..., FF_PAD:FF_PAD + D_FF]], axis=-1)


def _layer(l, h_p, h_s, cache_k, cache_v, state_pool, state_conv, page_table, p_prompt, p_sample,
           ln1, w_in, g_q, g_k, lam_q1, lam_k1, lam_q2, lam_k2, g_sub, w_pool, pool_scale, w_out,
           ln2, w_up, conv_w, conv_b, w_down, ln_ple, w_pg, w_pp):
    b, s, _ = h_p.shape
    n_req = h_s.shape[0]
    n_pool = cache_k.shape[1]
    lam_init = 0.8 - 0.6 * math.exp(-0.3 * l)
    sub_scale = 1.0 - lam_init

    row = lambda x: x[l].reshape(1, -1)
    win_b = w_in[l].astype(BF16)
    gq = jnp.tile(g_q[l], QK_WIDTH // HEAD_DIM).reshape(1, -1) * (HEAD_DIM ** -0.5)
    gk = jnp.tile(g_k[l], QK_WIDTH // HEAD_DIM).reshape(1, -1)
    grp = jnp.arange(MXU_DIM) // HEAD_DIM
    gmat = jnp.where(grp[:, None] == grp[None, :], 1.0 / HEAD_DIM, 0.0).astype(BF16)
    lamp = jnp.stack([lam_q1[l], lam_k1[l], lam_q2[l], lam_k2[l]])
    wup_b = w_up[l].astype(BF16)
    ffpad = ((0, 0), (0, FF_PAD - D_FF))
    tail_weights = (
        row(g_sub), w_pool[l].astype(BF16), row(pool_scale), w_out[l].astype(BF16), row(ln2),
        jnp.pad(wup_b[:, :D_FF], ffpad), jnp.pad(wup_b[:, D_FF:], ffpad),
        _split_ff(conv_w[l]), _split_ff(conv_b[l].reshape(1, -1)),
        jnp.pad(w_down[l].astype(BF16), ((0, FF_PAD - D_FF), (0, 0))),
        row(ln_ple), w_pg[l].astype(BF16), w_pp[l].astype(BF16),
    )

    qs, kts, v4s, pins, kbs, vbs = _project(h_s.reshape(n_req, D_MODEL), row(ln1), win_b, gmat, gq, gk,
                                            tm=n_req, seq=n_req)
    ck_view = jnp.transpose(cache_k[l], (0, 2, 3, 4, 1)).reshape(n_pool, QK_WIDTH, PAGE_SIZE)
    cv_view = cache_v[l].reshape(n_pool, PAGE_ROWS, V_DIM)
    stream_args = (qs.astype(F32), kbs.astype(F32), vbs.astype(F32), ck_view, cv_view)

    shp = lambda x: x.reshape(b, s, x.shape[-1])
    n_proj_req = n_req * HOST_SHARE[0] // sum(HOST_SHARE)
    n_attn_req = n_req * HOST_SHARE[1] // sum(HOST_SHARE)
    n_post_req = n_req - n_proj_req - n_attn_req
    q, kt, v4, pin, kb, vb, att_s0 = _project(
        h_p.reshape(b * s, D_MODEL), row(ln1), win_b, gmat, gq, gk, tm=PROJ_TILE, seq=s,
        host=(page_table, lamp, *stream_args, lam_init, 0, n_proj_req, STREAM_PAGES))
    n_mix = 4
    h1, att_s1 = _prompt_attention(page_table, lamp, shp(q), shp(kb), shp(vb), h_p, shp(pin), *stream_args,
                                   tail_weights[:n_mix], lam_init, sub_scale, blk=TOKEN_TILE,
                                   first_req=n_proj_req, n_host_req=n_attn_req, ppc=STREAM_PAGES)
    y_p, ctail, att_s2 = _post_prompt(page_table, lamp, h1, p_prompt[l], *stream_args,
                                      tail_weights[n_mix:], lam_init, tm=TOKEN_TILE,
                                      first_req=n_proj_req + n_attn_req, n_host_req=n_post_req,
                                      ppc=STREAM_PAGES)
    untranspose_k = lambda x, n: jnp.transpose(x.reshape(-1, N_HEADS, 2, HEAD_DIM, n), (0, 4, 1, 2, 3))
    k_p = untranspose_k(kt, s)
    v_p = v4.reshape(b, s, N_HEADS, V_DIM)
    pool_p = shp(pin)[:, s - POOL_BUF:]
    conv_p = _merge_ff(ctail[:, CONV_HALO - (CONV_W - 1):])

    att_s = jnp.concatenate([att_s0[:, 0], att_s1[:, 0], att_s2[:, 0]], axis=0)
    sconv = jnp.moveaxis(_split_ff(state_conv[l]), 1, 0)
    y_s, u_s = _post_sample(h_s.reshape(n_req, D_MODEL), att_s, pins, p_sample[l].reshape(n_req, PLE_DIM),
                            jnp.moveaxis(state_pool[l], 1, 0), sconv,
                            tail_weights, sub_scale)
    k_s = untranspose_k(kts, n_req).reshape(n_req, 1, N_HEADS, 2, HEAD_DIM)
    v_s = v4s.reshape(n_req, 1, N_HEADS, V_DIM)
    pool_s = jnp.concatenate([state_pool[l][:, 1:], pins[:, None]], axis=1)
    conv_s = jnp.concatenate([state_conv[l][:, 1:], _merge_ff(u_s)[:, None]], axis=1)

    return (y_p, y_s.reshape(n_req, 1, D_MODEL)), (k_p, v_p, pool_p, conv_p, k_s, v_s, pool_s, conv_s)


def kernel(x_prompt, x_sample, cache_k, cache_v, state_pool, state_conv, page_table, p_prompt, p_sample, ln1, w_in, g_q, g_k, lam_q1, lam_k1, lam_q2, lam_k2, g_sub, w_pool, pool_scale, w_out, ln2, w_up, conv_w, conv_b, w_down, ln_ple, w_pg, w_pp):
    depth = ln1.shape[0]
    h_p, h_s = x_prompt, x_sample
    per_layer = []
    for l in range(depth):
        (h_p, h_s), outs = _layer(
            l, h_p, h_s, cache_k, cache_v, state_pool, state_conv, page_table, p_prompt, p_sample,
            ln1, w_in, g_q, g_k, lam_q1, lam_k1, lam_q2, lam_k2, g_sub, w_pool, pool_scale, w_out,
            ln2, w_up, conv_w, conv_b, w_down, ln_ple, w_pg, w_pp)
        per_layer.append(outs)
    stacked = tuple(jnp.stack(xs) for xs in zip(*per_layer))
    return (h_p, h_s) + stacked
```

```python
import functools
import math

import jax
import jax.numpy as jnp
from jax import lax
from jax.experimental import pallas as pl
from jax.experimental.pallas import tpu as pltpu

F32 = jnp.float32
BF16 = jnp.bfloat16

D_MODEL = 1024
N_HEADS = 4
HEAD_DIM = 64
V_DIM = 2 * HEAD_DIM
QK_WIDTH = N_HEADS * 2 * HEAD_DIM
ATTN_WIDTH = N_HEADS * V_DIM
POOL_WINDOWS = (2, 4, 8, 16)
POOL_WIDTH = D_MODEL - ATTN_WIDTH
POOL_GROUP_DIM = POOL_WIDTH // len(POOL_WINDOWS)
POOL_BUF = max(POOL_WINDOWS) - 1
IN_WIDTH = 2 * QK_WIDTH + ATTN_WIDTH + POOL_WIDTH
D_FF = 2752
CONV_W = 3
PLE_DIM = 256
PAGE_SIZE = 128
PAGE_ROWS = PAGE_SIZE * N_HEADS
EPS = 1e-6
NEG = -1e30

LANES = 128
SUBLANES = 8
MXU_DIM = 256
N_DMA_PRIORITIES = 2
STREAM_AHEAD = 3
STREAM_SLOTS = STREAM_AHEAD + 2
VMEM_LIMIT_BYTES = 60 * 1024 * 1024

FF_PAD = -(-D_FF // MXU_DIM) * MXU_DIM
FF_CHUNK = MXU_DIM
TOKEN_TILE = 256
PROJ_TILE = 512
HOST_SHARE = (4, 5, 7)
STREAM_PAGES = 8
POOL_HALO = 16
CONV_HALO = SUBLANES


def _rms(x, g):
    ms = jnp.mean(x * x, axis=-1, keepdims=True)
    return x * lax.rsqrt(ms + EPS) * g


def _dot(a, b):
    return jnp.dot(a, b, preferred_element_type=F32)


def _dot_nt(a, b):
    return lax.dot_general(a, b, (((1,), (1,)), ((), ())), preferred_element_type=F32)


def _lambda(lamp_ref, lam_init):
    p = lamp_ref[...]
    s1 = jnp.sum(p[0:1] * p[1:2], axis=-1, keepdims=True)
    s2 = jnp.sum(p[2:3] * p[3:4], axis=-1, keepdims=True)
    return jnp.exp(s1) - jnp.exp(s2) + lam_init


def _proj_kernel(*refs, stream_cfg, lam_init):
    if stream_cfg is None:
        (h_ref, ln1_ref, win_ref, gmat_ref, gq_ref, gk_ref,
         q_ref, k_ref, v_ref, pin_ref, kb_ref, vb_ref, a_scr) = refs
        stream = None
    else:
        (pt_ref, h_ref, ln1_ref, win_ref, gmat_ref, gq_ref, gk_ref, lamp_ref,
         qs_ref, ks_ref, vs_ref, ck_hbm, cv_hbm,
         q_ref, k_ref, v_ref, pin_ref, kb_ref, vb_ref, os_ref, a_scr, *stream_scr) = refs
        step = pl.program_id(0)
        stream = _DecodeStream(pt_ref, (qs_ref, ks_ref, vs_ref), (ck_hbm, cv_hbm), os_ref, stream_scr,
                               _lambda(lamp_ref, lam_init), **stream_cfg)

        @pl.when(step == 0)
        def _():
            stream.prime()

        g0 = step * stream.chunks_per_step
        stream.fetch(g0)
        s_dec = stream.scores(g0)

    tm = h_ref.shape[0]
    a_scr[...] = _rms(h_ref[...], ln1_ref[...]).astype(BF16)
    gmat = gmat_ref[...]
    n_chunks = IN_WIDTH // MXU_DIM
    per_seg = QK_WIDTH // MXU_DIM
    for c in range(n_chunks):
        hosted = stream is not None and c < stream.chunks_per_step
        prefetch = stream is not None and c + 1 < stream.chunks_per_step
        if prefetch:
            stream.fetch(g0 + c + 1)
        z = _dot(a_scr[...], win_ref[:, c * MXU_DIM:(c + 1) * MXU_DIM])
        seg, off = divmod(c, per_seg)
        sl = slice(off * MXU_DIM, (off + 1) * MXU_DIM)
        if seg < 2:
            ms = _dot((z * z).astype(BF16), gmat)
            y = z * lax.rsqrt(ms + EPS)
            if seg == 0:
                q_ref[:, sl] = (y * gq_ref[:, sl]).astype(BF16)
            else:
                y = y * gk_ref[:, sl]
                k_ref[0, sl, :] = y.T
                kb_ref[:, sl] = y.astype(BF16)
        elif seg == 2:
            for hh in range(MXU_DIM // V_DIM):
                head = off * (MXU_DIM // V_DIM) + hh
                v_ref[pl.ds(head, tm, stride=N_HEADS), :] = z[:, hh * V_DIM:(hh + 1) * V_DIM]
            vb_ref[:, sl] = z.astype(BF16)
        else:
            pin_ref[:, sl] = z
        if prefetch:
            s_next = stream.scores(g0 + c + 1)
        if hosted:
            stream.update(g0 + c, s_dec)
            s_dec = s_next if prefetch else None


def _project(h2d, ln1, win_b, gmat, gq, gk, tm, seq, host=None):
    t = h2d.shape[0]
    assert t % seq == 0 and seq % tm == 0
    tiles = seq // tm
    n_steps = t // tm
    row = lambda i, *_: (i, 0)
    const = lambda i, *_: (0, 0)
    w512 = pl.BlockSpec((tm, QK_WIDTH), row)
    out_shape = [
        jax.ShapeDtypeStruct((t, QK_WIDTH), BF16),
        jax.ShapeDtypeStruct((t // seq, QK_WIDTH, seq), F32),
        jax.ShapeDtypeStruct((t * N_HEADS, V_DIM), F32),
        jax.ShapeDtypeStruct((t, POOL_WIDTH), F32),
        jax.ShapeDtypeStruct((t, QK_WIDTH), BF16),
        jax.ShapeDtypeStruct((t, ATTN_WIDTH), BF16),
    ]
    out_specs = [
        w512,
        pl.BlockSpec((1, QK_WIDTH, tm), lambda i, *_: (i // tiles, 0, i % tiles)),
        pl.BlockSpec((tm * N_HEADS, V_DIM), row),
        w512, w512, w512,
    ]
    in_specs = [
        pl.BlockSpec((tm, D_MODEL), row),
        pl.BlockSpec((1, D_MODEL), const),
        pl.BlockSpec((D_MODEL, IN_WIDTH), const, pipeline_mode=pl.Buffered(1)),
        pl.BlockSpec((MXU_DIM, MXU_DIM), const),
        pl.BlockSpec((1, QK_WIDTH), const),
        pl.BlockSpec((1, QK_WIDTH), const),
    ]
    scratch = [pltpu.VMEM((tm, D_MODEL), BF16)]
    args = (h2d, ln1, win_b, gmat, gq, gk)
    params = pltpu.CompilerParams(dimension_semantics=("arbitrary",), vmem_limit_bytes=VMEM_LIMIT_BYTES)
    if host is None:
        kernel = functools.partial(_proj_kernel, stream_cfg=None, lam_init=None)
        return pl.pallas_call(
            kernel, grid=(n_steps,), in_specs=in_specs, out_specs=out_specs, out_shape=out_shape,
            scratch_shapes=scratch, compiler_params=params, name="proj",
        )(*args)

    page_table, lamp, qs, ks_new, vs_new, cache_k, cache_v, lam_init, first_req, n_host_req, ppc = host
    stream_cfg = _stream_cfg(first_req, n_host_req, n_steps, page_table.shape[1], ppc)
    assert stream_cfg["chunks_per_step"] <= IN_WIDTH // MXU_DIM
    os_spec, os_shape = _stream_out(n_host_req)
    grid_spec = pltpu.PrefetchScalarGridSpec(
        num_scalar_prefetch=1,
        grid=(n_steps,),
        in_specs=in_specs + [pl.BlockSpec((4, HEAD_DIM), const)] + _stream_specs(qs.shape[0]),
        out_specs=out_specs + [os_spec],
        scratch_shapes=scratch + _stream_scratch(ppc),
    )
    kernel = functools.partial(_proj_kernel, stream_cfg=stream_cfg, lam_init=lam_init)
    return pl.pallas_call(
        kernel, grid_spec=grid_spec, out_shape=out_shape + [os_shape], compiler_params=params,
        name="proj_host",
    )(page_table.reshape(-1), *args, lamp, qs, ks_new, vs_new, cache_k, cache_v)


def _stream_scratch(ppc):
    return [
        pltpu.VMEM((STREAM_SLOTS, ppc, QK_WIDTH, PAGE_SIZE), F32),
        pltpu.VMEM((STREAM_SLOTS, ppc * PAGE_ROWS, V_DIM), F32),
        pltpu.SemaphoreType.DMA((STREAM_SLOTS, 2)),
        pltpu.VMEM((2, 2 * N_HEADS, QK_WIDTH), BF16),
        pltpu.VMEM((2 * N_HEADS, LANES), F32),
        pltpu.VMEM((2 * N_HEADS, LANES), F32),
        pltpu.VMEM((2 * N_HEADS, ATTN_WIDTH), F32),
    ]


class _DecodeStream:
    def __init__(self, pt_ref, rows, caches, out_ref, scratch, lam, *,
                 first_req, n_steps, n_pages, ppc, chunks_per_step):
        self.pt_ref, self.out_ref, self.lam = pt_ref, out_ref, lam
        self.q_ref, self.k_ref, self.v_ref = rows
        self.ck_hbm, self.cv_hbm = caches
        (self.kbuf, self.vbuf, self.sem, self.qbd, self.m, self.l, self.acc) = scratch
        self.first_req, self.n_pages, self.ppc = first_req, n_pages, ppc
        self.cpr = n_pages // ppc
        self.chunks_per_step = chunks_per_step
        self.total = n_steps * chunks_per_step
        assert self.total % self.cpr == 0
        self.hc = lax.broadcasted_iota(jnp.int32, (2 * N_HEADS, QK_WIDTH), 0)

    def _request(self, g):
        return self.first_req + lax.div(g, self.cpr)

    def _copies(self, g, slot, real):
        base = self._request(g) * self.n_pages + lax.rem(g, self.cpr) * self.ppc
        copies = []
        for p in range(self.ppc):
            page = self.pt_ref[base + p] if real else 0
            rows = pl.ds(p * PAGE_ROWS, PAGE_ROWS)
            copies.append(pltpu.make_async_copy(self.ck_hbm.at[page], self.kbuf.at[slot, p],
                                                self.sem.at[slot, 0]))
            copies.append(pltpu.make_async_copy(self.cv_hbm.at[page], self.vbuf.at[slot, rows],
                                                self.sem.at[slot, 1]))
        return copies

    def _start(self, g):
        for n, cp in enumerate(self._copies(g, lax.rem(g, STREAM_SLOTS), True)):
            cp.start(priority=n % N_DMA_PRIORITIES)

    def prime(self):
        for g in range(STREAM_AHEAD):
            self._start(jnp.int32(g))
        for ref in (self.m, self.l, self.acc):
            ref[...] = jnp.zeros(ref.shape, F32)

    def fetch(self, g):
        ahead = g + STREAM_AHEAD

        @pl.when(ahead < self.total)
        def _():
            self._start(ahead)

        for cp in self._copies(g, lax.rem(g, STREAM_SLOTS), False):
            cp.wait()

    def scores(self, g):
        req = self._request(g)
        parity = lax.rem(req, 2)
        grp = lax.broadcasted_iota(jnp.int32, self.hc.shape, 1) // HEAD_DIM
        q_row = self.q_ref[pl.ds(req, 1), :]
        qbd = jnp.where(self.hc == grp, jnp.broadcast_to(q_row, self.hc.shape), 0.0).astype(BF16)
        self.qbd[parity] = qbd
        slot = lax.rem(g, STREAM_SLOTS)
        return jnp.concatenate(
            [_dot(qbd, self.kbuf[slot, p].astype(BF16)) for p in range(self.ppc)], axis=1)

    def update(self, g, s):
        req = self._request(g)
        first = lax.rem(g, self.cpr) == 0
        qbd = self.qbd[lax.rem(req, 2)].astype(F32)
        s_self = jnp.sum(qbd * self.k_ref[pl.ds(req, 1), :], axis=-1, keepdims=True)
        m_prev = jnp.where(first, jnp.broadcast_to(s_self, self.m.shape), self.m[...])
        l_prev = jnp.where(first, 1.0, self.l[...])
        acc_prev = jnp.where(first, jnp.broadcast_to(self.v_ref[pl.ds(req, 1), :], self.acc.shape), self.acc[...])

        slot = lax.rem(g, STREAM_SLOTS)
        n_tok = self.ppc * PAGE_SIZE
        m_new = jnp.maximum(m_prev, jnp.max(s, axis=1, keepdims=True))
        alpha = jnp.exp(m_prev - m_new)
        p = jnp.exp(s - m_new[:, :1])
        l_new = alpha * l_prev + jnp.sum(p, axis=1, keepdims=True)
        pb = p.astype(BF16)
        pv = [_dot(pb, self.vbuf[slot, pl.ds(h, n_tok, stride=N_HEADS), :].astype(BF16))
              for h in range(N_HEADS)]
        acc_new = alpha[:, :1] * acc_prev + jnp.concatenate(pv, axis=1)
        self.l[...] = l_new
        self.acc[...] = acc_new
        self.m[...] = m_new

        coef = jnp.where(lax.rem(self.hc, 2) == 0, 1.0, -self.lam)
        head_of_lane = lax.broadcasted_iota(jnp.int32, self.hc.shape, 1) // V_DIM
        keep = head_of_lane == self.hc // 2
        o = jnp.where(keep, coef * acc_new / l_new[:, :1], 0.0)
        row = jnp.sum(o, axis=0, keepdims=True)
        self.out_ref[req - self.first_req] = jnp.broadcast_to(row, self.out_ref.shape[1:])


def _stream_specs(n_req):
    const = lambda *ids: (0, 0)
    rows = pl.BlockSpec((n_req, QK_WIDTH), const)
    return [rows, rows, rows, pl.BlockSpec(memory_space=pl.ANY), pl.BlockSpec(memory_space=pl.ANY)]


def _stream_out(n_host_req):
    return (pl.BlockSpec((n_host_req, SUBLANES, ATTN_WIDTH), lambda *ids: (0, 0, 0)),
            jax.ShapeDtypeStruct((n_host_req, SUBLANES, ATTN_WIDTH), F32))


def _stream_cfg(first_req, n_host_req, n_steps, n_pages, ppc):
    chunks = n_host_req * (n_pages // ppc)
    assert n_pages % ppc == 0 and chunks % n_steps == 0, "the hosted requests must fill the grid evenly"
    return dict(first_req=first_req, n_steps=n_steps, n_pages=n_pages, ppc=ppc,
                chunks_per_step=chunks // n_steps)


def _mix_residual(h, att, pooled, gsub_ref, wout_ref, sub_scale):
    parts = [_rms(att[:, hd * V_DIM:(hd + 1) * V_DIM], gsub_ref[...]) * sub_scale for hd in range(N_HEADS)]
    mixed = jnp.concatenate(parts + [pooled], axis=1).astype(BF16)
    return h + _dot(mixed, wout_ref[...])


def _pool_project(d_groups, wpool_ref, pscale_ref):
    ys = [_dot(d.astype(BF16), wpool_ref[g]) for g, d in enumerate(d_groups)]
    return jnp.concatenate(ys, axis=1) * pscale_ref[...]


def _gelu_gate(g, val):
    return 0.5 * g * (1.0 + lax.erf(g * (1.0 / math.sqrt(2.0)))) * val


def _ple(h, p, lnp_ref, wpg_ref, wpp_ref):
    gate = jax.nn.sigmoid(_dot(_rms(h, lnp_ref[...]).astype(BF16), wpg_ref[...]))
    return h + gate * _dot(p.astype(BF16), wpp_ref[...])


def _pool_mix(pin, tile_idx, pbuf, h, att, gsub_ref, wpool_ref, pscale_ref, wout_ref, sub_scale):
    tm = pin.shape[0]

    @pl.when(tile_idx == 0)
    def _():
        pbuf[0:POOL_HALO, :] = jnp.zeros((POOL_HALO, POOL_WIDTH), F32)

    pbuf[POOL_HALO:, :] = pin
    pos = tile_idx * tm + lax.broadcasted_iota(jnp.int32, (tm, 1), 0)
    d_groups = []
    for g, w in enumerate(POOL_WINDOWS):
        gs = slice(g * POOL_GROUP_DIM, (g + 1) * POOL_GROUP_DIM)
        tot = pin[:, gs]
        for back in range(1, w):
            tot = tot + pbuf[POOL_HALO - back:POOL_HALO - back + tm, gs]
        cnt = jnp.minimum(w, pos + 1).astype(F32)
        d_groups.append(tot / cnt - pin[:, gs])
    pbuf[0:POOL_HALO, :] = pbuf[tm:tm + POOL_HALO, :]
    pooled = _pool_project(d_groups, wpool_ref, pscale_ref)
    return _mix_residual(h, att, pooled, gsub_ref, wout_ref, sub_scale)


def _attn_kernel(pt_ref, lamp_ref, q_ref, k_ref, v_ref, h_ref, pin_ref, qs_ref, ks_ref, vs_ref, ck_hbm, cv_hbm,
                 gsub_ref, wpool_ref, pscale_ref, wout_ref,
                 h1_ref, os_ref, qq_scr, m_scr, l_scr, acc_scr, pbuf, *stream_scr,
                 blk, nq, lam_init, sub_scale, stream_cfg):
    i = pl.program_id(1)
    t = pl.program_id(0) * nq + i
    lam = _lambda(lamp_ref, lam_init)
    stream = _DecodeStream(pt_ref, (qs_ref, ks_ref, vs_ref), (ck_hbm, cv_hbm), os_ref, stream_scr, lam,
                           **stream_cfg)

    @pl.when(t == 0)
    def _():
        stream.prime()

    g0 = t * stream.chunks_per_step

    lane = lax.broadcasted_iota(jnp.int32, (blk, V_DIM), 1)
    row = lax.broadcasted_iota(jnp.int32, (2 * blk, blk), 0)
    col = lax.broadcasted_iota(jnp.int32, (2 * blk, blk), 1)
    causal = col <= jnp.where(row >= blk, row - blk, row)
    reps = blk // LANES
    heads = [slice(h * V_DIM, (h + 1) * V_DIM) for h in range(N_HEADS)]

    for h, hs in enumerate(heads):
        qh = q_ref[0, :, hs]
        zero = jnp.zeros_like(qh)
        qq_scr[h, 0:blk, :] = jnp.where(lane < HEAD_DIM, qh, zero)
        qq_scr[h, blk:2 * blk, :] = jnp.where(lane >= HEAD_DIM, qh, zero)
    m_scr[...] = jnp.full(m_scr.shape, -jnp.inf, F32)
    l_scr[...] = jnp.zeros(l_scr.shape, F32)
    acc_scr[...] = jnp.zeros(acc_scr.shape, F32)

    def attn_scores(j, masked, which):
        start = pl.multiple_of(j * blk, blk)
        out = []
        for h in which:
            s = _dot_nt(qq_scr[h], k_ref[0, pl.ds(start, blk), heads[h]])
            out.append(jnp.where(causal, s, NEG) if masked else s)
        return out

    def attn_update(j, scores, which):
        start = pl.multiple_of(j * blk, blk)
        for s, h in zip(scores, which):
            hs = heads[h]
            m_prev = m_scr[h]
            m_new = jnp.maximum(m_prev, jnp.max(s, axis=1, keepdims=True))
            alpha = jnp.exp(m_prev - m_new)
            p = jnp.exp(s - jnp.concatenate([m_new] * reps, axis=1))
            l_scr[h] = alpha * l_scr[h] + jnp.sum(p, axis=1, keepdims=True)
            acc_scr[h] = alpha * acc_scr[h] + _dot(p.astype(BF16), v_ref[0, pl.ds(start, blk), hs])
            m_scr[h] = m_new

    stream.fetch(g0)
    s0 = stream.scores(g0)

    def prompt_block(j, between=lambda: None, after=lambda: None):
        half = N_HEADS // 2
        first = attn_scores(j, False, range(half))
        between()
        attn_update(j, first, range(half))
        second = attn_scores(j, False, range(half, N_HEADS))
        after()
        attn_update(j, second, range(half, N_HEADS))

    def body(j, s):
        stream.fetch(g0 + j + 1)
        nxt = []
        prompt_block(j, lambda: nxt.append(stream.scores(g0 + j + 1)), lambda: stream.update(g0 + j, s))
        return nxt[0]

    def plain(j, carry):
        prompt_block(j)
        return carry

    def rest(c, s):
        stream.fetch(g0 + c + 1)
        s_next = stream.scores(g0 + c + 1)
        stream.update(g0 + c, s)
        return s_next

    last = stream.chunks_per_step - 1
    n_hosted = jnp.minimum(i, last)
    s_dec = lax.fori_loop(0, n_hosted, body, s0)
    lax.fori_loop(n_hosted, i, plain, 0)
    attn_update(i, attn_scores(i, True, range(N_HEADS)), range(N_HEADS))
    s_dec = lax.fori_loop(n_hosted, last, rest, s_dec)
    stream.update(g0 + last, s_dec)
    att = []
    for h in range(N_HEADS):
        o = acc_scr[h] / l_scr[h]
        att.append(o[:blk] - lam * o[blk:])
    h1_ref[0] = _pool_mix(pin_ref[0], i, pbuf, h_ref[0], jnp.concatenate(att, axis=1),
                          gsub_ref, wpool_ref, pscale_ref, wout_ref, sub_scale)


def _prompt_attention(page_table, lamp, q, kb, vb, h, pin, qs, ks_new, vs_new, cache_k, cache_v,
                      mix_weights, lam_init, sub_scale, blk, first_req, n_host_req, ppc):
    b, s, _ = q.shape
    assert s % blk == 0
    nq = s // blk
    stream_cfg = _stream_cfg(first_req, n_host_req, b * nq, page_table.shape[1], ppc)
    kernel = functools.partial(_attn_kernel, blk=blk, nq=nq, lam_init=lam_init, sub_scale=sub_scale,
                               stream_cfg=stream_cfg)
    os_spec, os_shape = _stream_out(n_host_req)
    tile = lambda width: pl.BlockSpec((1, blk, width), lambda bi, i, pt: (bi, i, 0))
    resident = lambda arr: pl.BlockSpec(arr.shape, lambda bi, i, pt: (0,) * arr.ndim,
                                        pipeline_mode=pl.Buffered(1))
    grid_spec = pltpu.PrefetchScalarGridSpec(
        num_scalar_prefetch=1,
        grid=(b, nq),
        in_specs=[
            pl.BlockSpec((4, HEAD_DIM), lambda bi, i, pt: (0, 0)),
            tile(QK_WIDTH),
            pl.BlockSpec((1, s, QK_WIDTH), lambda bi, i, pt: (bi, 0, 0)),
            pl.BlockSpec((1, s, ATTN_WIDTH), lambda bi, i, pt: (bi, 0, 0)),
            tile(D_MODEL), tile(POOL_WIDTH),
        ] + _stream_specs(qs.shape[0]) + [resident(w) for w in mix_weights],
        out_specs=(tile(D_MODEL), os_spec),
        scratch_shapes=[pltpu.VMEM((N_HEADS, 2 * blk, V_DIM), BF16)]
        + [pltpu.VMEM((N_HEADS, 2 * blk, V_DIM), F32)] * 3
        + [pltpu.VMEM((blk + POOL_HALO, POOL_WIDTH), F32)] + _stream_scratch(ppc),
    )
    return pl.pallas_call(
        kernel,
        grid_spec=grid_spec,
        out_shape=(jax.ShapeDtypeStruct((b, s, D_MODEL), F32), os_shape),
        compiler_params=pltpu.CompilerParams(
            dimension_semantics=("arbitrary", "arbitrary"), vmem_limit_bytes=VMEM_LIMIT_BYTES),
        name="prompt_attn",
    )(page_table.reshape(-1), lamp, q, kb, vb, h, pin, qs, ks_new, vs_new, cache_k, cache_v, *mix_weights)


def _post_prompt_kernel(pt_ref, lamp_ref, h1_ref, p_ref,
                        qs_ref, ks_ref, vs_ref, ck_hbm, cv_hbm,
                        ln2_ref, wug_ref, wuv_ref, cw_ref, cb_ref, wdn_ref, lnp_ref, wpg_ref, wpp_ref,
                        y_ref, ctail_ref, os_ref,
                        ubuf_g, ubuf_v, ucarry, a2_scr, act_scr, *stream_scr,
                        tm, n_tiles, lam_init, stream_cfg):
    t = pl.program_id(1)
    step = pl.program_id(0) * n_tiles + t
    stream = _DecodeStream(pt_ref, (qs_ref, ks_ref, vs_ref), (ck_hbm, cv_hbm), os_ref, stream_scr,
                           _lambda(lamp_ref, lam_init), **stream_cfg)

    @pl.when(step == 0)
    def _():
        stream.prime()

    g0 = step * stream.chunks_per_step
    stream.fetch(g0)
    s_dec = stream.scores(g0)

    @pl.when(t == 0)
    def _():
        ucarry[...] = jnp.zeros(ucarry.shape, F32)

    h1 = h1_ref[0]

    a2_scr[...] = _rms(h1, ln2_ref[...]).astype(BF16)
    halves = ((wug_ref, ubuf_g), (wuv_ref, ubuf_v))

    def columns(half, c):
        return slice(half * FF_PAD + c * FF_CHUNK, half * FF_PAD + (c + 1) * FF_CHUNK)

    def ffn_up(c):
        for half, (w_ref, ubufs) in enumerate(halves):
            ubuf = ubufs.at[c % 2]
            ubuf[0:CONV_HALO, :] = ucarry[:, columns(half, c)]
            ubuf[CONV_HALO:, :] = _dot(a2_scr[...], w_ref[:, c * FF_CHUNK:(c + 1) * FF_CHUNK])

    def ffn_activate(c):
        conv = []
        for half, (_, ubufs) in enumerate(halves):
            ubuf = ubufs.at[c % 2]
            hs = columns(half, c)
            acc = cb_ref[:, hs]
            for j in range(CONV_W):
                lo = CONV_HALO - (CONV_W - 1) + j
                acc = acc + ubuf[lo:lo + tm, :] * cw_ref[j:j + 1, hs]
            ucarry[:, hs] = ubuf[tm:tm + CONV_HALO, :]
            conv.append(acc)
        act_scr[:, c * FF_CHUNK:(c + 1) * FF_CHUNK] = _gelu_gate(conv[0], conv[1]).astype(BF16)

    n_ffn = FF_PAD // FF_CHUNK
    ffn_up(0)
    for c in range(n_ffn):
        nxt = c + 1
        if nxt < stream.chunks_per_step:
            stream.fetch(g0 + nxt)
        if nxt < n_ffn:
            ffn_up(nxt)
        if nxt < stream.chunks_per_step:
            s_next = stream.scores(g0 + nxt)
        ffn_activate(c)
        if c < stream.chunks_per_step:
            stream.update(g0 + c, s_dec)
            s_dec = s_next
    ctail_ref[0] = ucarry[...]
    h2 = h1 + _dot(act_scr[...], wdn_ref[...])

    y_ref[0] = _ple(h2, p_ref[0], lnp_ref, wpg_ref, wpp_ref)


def _post_prompt(page_table, lamp, h1, p, qs, ks_new, vs_new, cache_k, cache_v, ffn_weights,
                 lam_init, tm, first_req, n_host_req, ppc):
    b, s, _ = h1.shape
    assert s % tm == 0
    n_tiles = s // tm
    tile = lambda width: pl.BlockSpec((1, tm, width), lambda bi, t, pt: (bi, t, 0))
    resident = lambda arr: pl.BlockSpec(arr.shape, lambda bi, t, pt: (0,) * arr.ndim,
                                        pipeline_mode=pl.Buffered(1))
    stream_cfg = _stream_cfg(first_req, n_host_req, b * n_tiles, page_table.shape[1], ppc)
    assert stream_cfg["chunks_per_step"] <= FF_PAD // FF_CHUNK
    kernel = functools.partial(_post_prompt_kernel, tm=tm, n_tiles=n_tiles, lam_init=lam_init,
                               stream_cfg=stream_cfg)
    os_spec, os_shape = _stream_out(n_host_req)
    grid_spec = pltpu.PrefetchScalarGridSpec(
        num_scalar_prefetch=1,
        grid=(b, n_tiles),
        in_specs=[pl.BlockSpec((4, HEAD_DIM), lambda bi, t, pt: (0, 0)), tile(D_MODEL), tile(PLE_DIM)]
        + _stream_specs(qs.shape[0]) + [resident(w) for w in ffn_weights],
        out_specs=(tile(D_MODEL),
                   pl.BlockSpec((1, CONV_HALO, 2 * FF_PAD), lambda bi, t, pt: (bi, 0, 0)),
                   os_spec),
        scratch_shapes=[
            pltpu.VMEM((2, tm + CONV_HALO, FF_CHUNK), F32),
            pltpu.VMEM((2, tm + CONV_HALO, FF_CHUNK), F32),
            pltpu.VMEM((CONV_HALO, 2 * FF_PAD), F32),
            pltpu.VMEM((tm, D_MODEL), BF16),
            pltpu.VMEM((tm, FF_PAD), BF16),
        ] + _stream_scratch(ppc),
    )
    return pl.pallas_call(
        kernel,
        grid_spec=grid_spec,
        out_shape=(jax.ShapeDtypeStruct((b, s, D_MODEL), F32),
                   jax.ShapeDtypeStruct((b, CONV_HALO, 2 * FF_PAD), F32),
                   os_shape),
        compiler_params=pltpu.CompilerParams(
            dimension_semantics=("arbitrary", "arbitrary"), vmem_limit_bytes=VMEM_LIMIT_BYTES),
        name="post_prompt",
    )(page_table.reshape(-1), lamp, h1, p, qs, ks_new, vs_new, cache_k, cache_v, *ffn_weights)


def _post_sample_kernel(h_ref, att_ref, pin_ref, p_ref, spool_ref, sconv_ref,
                        gsub_ref, wpool_ref, pscale_ref, wout_ref, ln2_ref,
                        wug_ref, wuv_ref, cw_ref, cb_ref, wdn_ref, lnp_ref, wpg_ref, wpp_ref,
                        y_ref, u_ref, a2_scr, act_scr, *, sub_scale):
    pin = pin_ref[...]
    d_groups = []
    for g, w in enumerate(POOL_WINDOWS):
        gs = slice(g * POOL_GROUP_DIM, (g + 1) * POOL_GROUP_DIM)
        tot = pin[:, gs]
        for back in range(1, w):
            tot = tot + spool_ref[POOL_BUF - back, :, gs]
        d_groups.append(tot / float(w) - pin[:, gs])
    pooled = _pool_project(d_groups, wpool_ref, pscale_ref)

    h1 = _mix_residual(h_ref[...], att_ref[...], pooled, gsub_ref, wout_ref, sub_scale)

    a2_scr[...] = _rms(h1, ln2_ref[...]).astype(BF16)
    for c in range(FF_PAD // FF_CHUNK):
        cs = slice(c * FF_CHUNK, (c + 1) * FF_CHUNK)
        conv = []
        for half, w_ref in enumerate((wug_ref, wuv_ref)):
            hs = slice(half * FF_PAD + c * FF_CHUNK, half * FF_PAD + (c + 1) * FF_CHUNK)
            u = _dot(a2_scr[...], w_ref[:, cs])
            u_ref[:, hs] = u
            acc = cb_ref[:, hs] + u * cw_ref[CONV_W - 1:CONV_W, hs]
            for j in range(CONV_W - 1):
                acc = acc + sconv_ref[j, :, hs] * cw_ref[j:j + 1, hs]
            conv.append(acc)
        act_scr[:, cs] = _gelu_gate(conv[0], conv[1]).astype(BF16)
    h2 = h1 + _dot(act_scr[...], wdn_ref[...])

    y_ref[...] = _ple(h2, p_ref[...], lnp_ref, wpg_ref, wpp_ref)


def _post_sample(h, att, pin, p, spool, sconv, weights, sub_scale):
    n = h.shape[0]
    args = (h, att, pin, p, spool, sconv) + tuple(weights)
    full = lambda arr: pl.BlockSpec(arr.shape, lambda i: (0,) * arr.ndim, pipeline_mode=pl.Buffered(1))
    kernel = functools.partial(_post_sample_kernel, sub_scale=sub_scale)
    return pl.pallas_call(
        kernel,
        grid=(1,),
        in_specs=[full(a) for a in args],
        out_specs=(pl.BlockSpec((n, D_MODEL), lambda i: (0, 0)),
                   pl.BlockSpec((n, 2 * FF_PAD), lambda i: (0, 0))),
        out_shape=(jax.ShapeDtypeStruct((n, D_MODEL), F32),
                   jax.ShapeDtypeStruct((n, 2 * FF_PAD), F32)),
        scratch_shapes=[pltpu.VMEM((n, D_MODEL), BF16), pltpu.VMEM((n, FF_PAD), BF16)],
        compiler_params=pltpu.CompilerParams(
            dimension_semantics=("arbitrary",), vmem_limit_bytes=VMEM_LIMIT_BYTES),
        name="post_sample",
    )(*args)


def _split_ff(x):
    pad = [(0, 0)] * (x.ndim - 1) + [(0, FF_PAD - D_FF)]
    return jnp.concatenate([jnp.pad(x[..., :D_FF], pad), jnp.pad(x[..., D_FF:], pad)], axis=-1)


def _merge_ff(x):
    return jnp.concatenate([x[..., :D_FF], x[..., FF_PAD:FF_PAD + D_FF]], axis=-1)


def _layer(l, h_p, h_s, cache_k, cache_v, state_pool, state_conv, page_table, p_prompt, p_sample,
           ln1, w_in, g_q, g_k, lam_q1, lam_k1, lam_q2, lam_k2, g_sub, w_pool, pool_scale, w_out,
           ln2, w_up, conv_w, conv_b, w_down, ln_ple, w_pg, w_pp):
    b, s, _ = h_p.shape
    n_req = h_s.shape[0]
    n_pool = cache_k.shape[1]
    lam_init = 0.8 - 0.6 * math.exp(-0.3 * l)
    sub_scale = 1.0 - lam_init

    row = lambda x: x[l].reshape(1, -1)
    win_b = w_in[l].astype(BF16)
    gq = jnp.tile(g_q[l], QK_WIDTH // HEAD_DIM).reshape(1, -1) * (HEAD_DIM ** -0.5)
    gk = jnp.tile(g_k[l], QK_WIDTH // HEAD_DIM).reshape(1, -1)
    grp = jnp.arange(MXU_DIM) // HEAD_DIM
    gmat = jnp.where(grp[:, None] == grp[None, :], 1.0 / HEAD_DIM, 0.0).astype(BF16)
    lamp = jnp.stack([lam_q1[l], lam_k1[l], lam_q2[l], lam_k2[l]])
    wup_b = w_up[l].astype(BF16)
    ffpad = ((0, 0), (0, FF_PAD - D_FF))
    tail_weights = (
        row(g_sub), w_pool[l].astype(BF16), row(pool_scale), w_out[l].astype(BF16), row(ln2),
        jnp.pad(wup_b[:, :D_FF], ffpad), jnp.pad(wup_b[:, D_FF:], ffpad),
        _split_ff(conv_w[l]), _split_ff(conv_b[l].reshape(1, -1)),
        jnp.pad(w_down[l].astype(BF16), ((0, FF_PAD - D_FF), (0, 0))),
        row(ln_ple), w_pg[l].astype(BF16), w_pp[l].astype(BF16),
    )

    qs, kts, v4s, pins, kbs, vbs = _project(h_s.reshape(n_req, D_MODEL), row(ln1), win_b, gmat, gq, gk,
                                            tm=n_req, seq=n_req)
    ck_view = jnp.transpose(cache_k[l], (0, 2, 3, 4, 1)).reshape(n_pool, QK_WIDTH, PAGE_SIZE)
    cv_view = cache_v[l].reshape(n_pool, PAGE_ROWS, V_DIM)
    stream_args = (qs.astype(F32), kbs.astype(F32), vbs.astype(F32), ck_view, cv_view)

    shp = lambda x: x.reshape(b, s, x.shape[-1])
    n_proj_req = n_req * HOST_SHARE[0] // sum(HOST_SHARE)
    n_attn_req = n_req * HOST_SHARE[1] // sum(HOST_SHARE)
    n_post_req = n_req - n_proj_req - n_attn_req
    q, kt, v4, pin, kb, vb, att_s0 = _project(
        h_p.reshape(b * s, D_MODEL), row(ln1), win_b, gmat, gq, gk, tm=PROJ_TILE, seq=s,
        host=(page_table, lamp, *stream_args, lam_init, 0, n_proj_req, STREAM_PAGES))
    n_mix = 4
    h1, att_s1 = _prompt_attention(page_table, lamp, shp(q), shp(kb), shp(vb), h_p, shp(pin), *stream_args,
                                   tail_weights[:n_mix], lam_init, sub_scale, blk=TOKEN_TILE,
                                   first_req=n_proj_req, n_host_req=n_attn_req, ppc=STREAM_PAGES)
    y_p, ctail, att_s2 = _post_prompt(page_table, lamp, h1, p_prompt[l], *stream_args,
                                      tail_weights[n_mix:], lam_init, tm=TOKEN_TILE,
                                      first_req=n_proj_req + n_attn_req, n_host_req=n_post_req,
                                      ppc=STREAM_PAGES)
    untranspose_k = lambda x, n: jnp.transpose(x.reshape(-1, N_HEADS, 2, HEAD_DIM, n), (0, 4, 1, 2, 3))
    k_p = untranspose_k(kt, s)
    v_p = v4.reshape(b, s, N_HEADS, V_DIM)
    pool_p = shp(pin)[:, s - POOL_BUF:]
    conv_p = _merge_ff(ctail[:, CONV_HALO - (CONV_W - 1):])

    att_s = jnp.concatenate([att_s0[:, 0], att_s1[:, 0], att_s2[:, 0]], axis=0)
    sconv = jnp.moveaxis(_split_ff(state_conv[l]), 1, 0)
    y_s, u_s = _post_sample(h_s.reshape(n_req, D_MODEL), att_s, pins, p_sample[l].reshape(n_req, PLE_DIM),
                            jnp.moveaxis(state_pool[l], 1, 0), sconv,
                            tail_weights, sub_scale)
    k_s = untranspose_k(kts, n_req).reshape(n_req, 1, N_HEADS, 2, HEAD_DIM)
    v_s = v4s.reshape(n_req, 1, N_HEADS, V_DIM)
    pool_s = jnp.concatenate([state_pool[l][:, 1:], pins[:, None]], axis=1)
    conv_s = jnp.concatenate([state_conv[l][:, 1:], _merge_ff(u_s)[:, None]], axis=1)

    return (y_p, y_s.reshape(n_req, 1, D_MODEL)), (k_p, v_p, pool_p, conv_p, k_s, v_s, pool_s, conv_s)


def kernel(x_prompt, x_sample, cache_k, cache_v, state_pool, state_conv, page_table, p_prompt, p_sample, ln1, w_in, g_q, g_k, lam_q1, lam_k1, lam_q2, lam_k2, g_sub, w_pool, pool_scale, w_out, ln2, w_up, conv_w, conv_b, w_down, ln_ple, w_pg, w_pp):
    depth = ln1.shape[0]
    h_p, h_s = x_prompt, x_sample
    per_layer = []
    for l in range(depth):
        (h_p, h_s), outs = _layer(
            l, h_p, h_s, cache_k, cache_v, state_pool, state_conv, page_table, p_prompt, p_sample,
            ln1, w_in, g_q, g_k, lam_q1, lam_k1, lam_q2, lam_k2, g_sub, w_pool, pool_scale, w_out,
            ln2, w_up, conv_w, conv_b, w_down, ln_ple, w_pg, w_pp)
        per_layer.append(outs)
    stacked = tuple(jnp.stack(xs) for xs in zip(*per_layer))
    return (h_p, h_s) + stacked
```
